```python
import math
import jax, jax.numpy as jnp
from jax import lax
import numpy as np

D_MODEL = 4096
BATCH = 1
SEQ = 16384
DEPTH = 4

CHUNK = 64
N_META = 16
N_MIXERS = 3
NORM_EPS = 1e-6
N_LAYERS_A = (DEPTH + 2) // 3
N_LAYERS_B = (DEPTH + 1) // 3
N_LAYERS_C = DEPTH // 3

A_HEAD_DIM = 128
A_HEADS = D_MODEL // A_HEAD_DIM

B_QK_DIM = 128
B_HEADS = D_MODEL // (2 * B_QK_DIM)
B_V_DIM = 2 * B_QK_DIM
ROPE_THETA = 500000.0
ROPE_DIM = B_QK_DIM // 4
Q_BLOCK = 128
SUBLN_EPS = 1e-5

C_HEAD_DIM = 64
C_HEADS = D_MODEL // C_HEAD_DIM
C_DECAY_LORA = 128
C_AAA_LORA = 128
C_DECAY_SCALE = 0.606531
LNX_EPS = 64e-5

kernel_name = 'hybrid_hgrn2_diffattn_rwkv7_trunk'

F32 = jnp.float32


def rmsnorm(x, g, eps=NORM_EPS):
    xf = x.astype(F32)
    y = xf * lax.rsqrt(jnp.mean(xf * xf, axis=-1, keepdims=True) + eps)
    return (y * g.astype(F32)).astype(x.dtype)


def chunk_id(pos):
    return jnp.where(pos < N_META, 0, 1 + (pos - N_META) // CHUNK)


def partial_rope(x, cos, sin):
    half = ROPE_DIM // 2
    x1, x2, rest = x[..., :half], x[..., half:ROPE_DIM], x[..., ROPE_DIM:]
    return jnp.concatenate([(x1 * cos - x2 * sin).astype(x.dtype),
                            (x2 * cos + x1 * sin).astype(x.dtype), rest], axis=-1)


def hgrn2_mixer(u, w_in, lb, gnorm_g, w_out):
    bsz, seqlen, d = u.shape
    q, f_logit, i_in, gate = jnp.split(u @ w_in, 4, axis=-1)
    f = lb + (1.0 - lb) * jax.nn.sigmoid(f_logit.astype(F32))
    log_f = jnp.log(f)
    k = 1.0 - f
    pad = (-seqlen) % CHUNK
    n_chunks = (seqlen + pad) // CHUNK

    def to_chunks(t):
        t = jnp.pad(t.astype(F32), ((0, 0), (pad, 0), (0, 0)))
        return t.reshape(bsz, n_chunks, CHUNK, A_HEADS, A_HEAD_DIM).transpose(1, 0, 3, 2, 4)

    qc, kc, vc, gc = (to_chunks(t) for t in (q, k, i_in, log_f))
    causal = jnp.tril(jnp.ones((CHUNK, CHUNK), dtype=bool))[:, :, None]

    def step(state, inp):
        qb, kb, vb, gb = inp
        b = jnp.cumsum(gb, axis=2)
        b_last = b[:, :, -1:, :]
        rel = jnp.where(causal, b[:, :, :, None, :] - b[:, :, None, :, :], -jnp.inf)
        scores = jnp.einsum('bhtd,bhsd,bhtsd->bhts', qb, kb, jnp.exp(rel))
        out = (jnp.einsum('bhts,bhse->bhte', scores, vb)
               + jnp.einsum('bhtd,bhde->bhte', qb * jnp.exp(b), state))
        state = (jnp.exp(b_last[:, :, 0, :, None]) * state
                 + jnp.einsum('bhsd,bhse->bhde', kb * jnp.exp(b_last - b), vb))
        return state, out

    s0 = jnp.zeros((bsz, A_HEADS, A_HEAD_DIM, A_HEAD_DIM), F32)
    _, o = lax.scan(step, s0, (qc, kc, vc, gc))
    o = o.transpose(1, 0, 3, 2, 4).reshape(bsz, n_chunks * CHUNK, A_HEADS, A_HEAD_DIM)[:, pad:]
    o = rmsnorm(o, gnorm_g).reshape(bsz, seqlen, d)
    return (o * jax.nn.silu(gate.astype(F32))).astype(u.dtype) @ w_out


def diff_attn_mixer(u, w_in, lam_q1, lam_k1, lam_q2, lam_k2, subln_g, w_out,
                    lambda_init, cos, sin, kcid):
    bsz, seqlen, d = u.shape
    q, k, v, gate = jnp.split(u @ w_in, 4, axis=-1)
    q = partial_rope(q.reshape(bsz, seqlen, B_HEADS, 2, B_QK_DIM), cos, sin)
    k = partial_rope(k.reshape(bsz, seqlen, B_HEADS, 2, B_QK_DIM), cos, sin)
    k = k.transpose(0, 2, 3, 1, 4)
    v = v.reshape(bsz, seqlen, B_HEADS, B_V_DIM).transpose(0, 2, 1, 3)
    lam = (jnp.exp(jnp.dot(lam_q1.astype(F32), lam_k1.astype(F32)))
           - jnp.exp(jnp.dot(lam_q2.astype(F32), lam_k2.astype(F32))) + lambda_init)
    n_blocks = -(-seqlen // Q_BLOCK)
    q_len = n_blocks * Q_BLOCK
    q = jnp.pad(q, ((0, 0), (0, q_len - seqlen), (0, 0), (0, 0), (0, 0)))
    q = q.reshape(bsz, n_blocks, Q_BLOCK, B_HEADS, 2, B_QK_DIM).transpose(1, 0, 3, 4, 2, 5)
    qcid = chunk_id(jnp.arange(q_len)).reshape(n_blocks, Q_BLOCK)
    scale = B_QK_DIM ** -0.5

    def attend(args):
        qb, qc = args
        s = jnp.einsum('bhcqd,bhckd->bhcqk', qb, k).astype(F32) * scale
        s = jnp.where(kcid[None, :] <= qc[:, None], s, -jnp.inf)
        p = jax.nn.softmax(s, axis=-1)
        a = p[:, :, 0] - lam * p[:, :, 1]
        return jnp.einsum('bhqk,bhke->bhqe', a.astype(v.dtype), v)

    o = lax.map(attend, (q, qcid))
    o = o.transpose(1, 0, 3, 2, 4).reshape(bsz, q_len, B_HEADS, B_V_DIM)[:, :seqlen]
    o = (rmsnorm(o, subln_g, SUBLN_EPS) * (1.0 - lambda_init)).reshape(bsz, seqlen, d)
    return (o * jax.nn.silu(gate)).astype(u.dtype) @ w_out


def rwkv7_mixer(u, mu, w_in, w0, w1, w2, a0, a1, a2, k_k, k_a, r_k, lnx_w, lnx_b, w_out):
    bsz, seqlen, d = u.shape
    delta = jnp.pad(u, ((0, 0), (1, 0), (0, 0)))[:, :-1] - u

    def mixed(s):
        return u + delta * mu[s]

    r, k, v, gate = (mixed(s) @ w_in[:, s * d:(s + 1) * d] for s in range(4))
    decay = jnp.exp(-C_DECAY_SCALE * jax.nn.sigmoid((w0 + jnp.tanh(mixed(4) @ w1) @ w2).astype(F32)))
    a = jax.nn.sigmoid((a0 + (mixed(5) @ a1) @ a2).astype(F32))

    def heads(t):
        return t.astype(F32).reshape(bsz, seqlen, C_HEADS, C_HEAD_DIM)

    r, k, v, decay, a = (heads(t) for t in (r, k, v, decay, a))
    kk = k * k_k.astype(F32).reshape(C_HEADS, C_HEAD_DIM)
    kk = kk / jnp.maximum(jnp.sqrt(jnp.sum(kk * kk, axis=-1, keepdims=True)), 1e-12)
    k = k * (1.0 + (a - 1.0) * k_a.astype(F32).reshape(C_HEADS, C_HEAD_DIM))

    def step(state, inp):
        r_t, w_t, k_t, v_t, kk_t, a_t = inp
        sa = jnp.einsum('bhij,bhj->bhi', state, -kk_t)
        state = (state * w_t[:, :, None, :] + sa[..., None] * (kk_t * a_t)[:, :, None, :]
                 + v_t[..., None] * k_t[:, :, None, :])
        return state, jnp.einsum('bhij,bhj->bhi', state, r_t)

    def seq_first(t):
        return t.transpose(1, 0, 2, 3)

    s0 = jnp.zeros((bsz, C_HEADS, C_HEAD_DIM, C_HEAD_DIM), F32)
    _, o = lax.scan(step, s0, tuple(seq_first(t) for t in (r, decay, k, v, kk, a)))
    o = seq_first(o)
    mean = jnp.mean(o, axis=-1, keepdims=True)
    var = jnp.mean(jnp.square(o - mean), axis=-1, keepdims=True)
    o = ((o - mean) * lax.rsqrt(var + LNX_EPS)).reshape(bsz, seqlen, d) * lnx_w.astype(F32) + lnx_b.astype(F32)
    bonus = jnp.sum(r * k * r_k.astype(F32), axis=-1, keepdims=True) * v
    o = o + bonus.reshape(bsz, seqlen, d)
    return (o * jax.nn.silu(gate.astype(F32))).astype(u.dtype) @ w_out


def setup_inputs(seed: int = 0) -> dict:
    key = jax.random.key(seed)
    ks = iter(jax.random.split(key, 32))
    d = D_MODEL
    s = d ** -0.5

    def nrm(shape, scale):
        return jax.random.normal(next(ks), shape, F32) * scale

    inp = {}
    inp['x'] = nrm((BATCH, SEQ, d), 1.0)
    inp['meta_tokens'] = nrm((N_META, d), 1.0)
    inp['pre_norm_g'] = 1.0 + nrm((DEPTH, d), 0.02)
    inp['post_norm_g'] = 1.0 + nrm((DEPTH, d), 0.02)
    inp['a_w_in'] = nrm((N_LAYERS_A, d, 4 * d), s)
    inp['a_lb_logits'] = nrm((N_LAYERS_A, d), 0.5)
    inp['a_gnorm_g'] = 1.0 + nrm((N_LAYERS_A, A_HEAD_DIM), 0.02)
    inp['a_w_out'] = nrm((N_LAYERS_A, d, d), s)
    inp['b_w_in'] = nrm((N_LAYERS_B, d, 4 * d), s)
    inp['b_lam_q1'] = nrm((N_LAYERS_B, B_QK_DIM), 0.1)
    inp['b_lam_k1'] = nrm((N_LAYERS_B, B_QK_DIM), 0.1)
    inp['b_lam_q2'] = nrm((N_LAYERS_B, B_QK_DIM), 0.1)
    inp['b_lam_k2'] = nrm((N_LAYERS_B, B_QK_DIM), 0.1)
    inp['b_subln_g'] = 1.0 + nrm((N_LAYERS_B, B_V_DIM), 0.02)
    inp['b_w_out'] = nrm((N_LAYERS_B, d, d), s)
    inp['c_mu'] = jax.random.uniform(next(ks), (N_LAYERS_C, 6, d), F32)
    inp['c_w_in'] = nrm((N_LAYERS_C, d, 4 * d), s)
    inp['c_w0'] = nrm((N_LAYERS_C, d), 1.0)
    inp['c_w1'] = nrm((N_LAYERS_C, d, C_DECAY_LORA), s)
    inp['c_w2'] = nrm((N_LAYERS_C, C_DECAY_LORA, d), 0.1 * C_DECAY_LORA ** -0.5)
    inp['c_a0'] = nrm((N_LAYERS_C, d), 0.1)
    inp['c_a1'] = nrm((N_LAYERS_C, d, C_AAA_LORA), s)
    inp['c_a2'] = nrm((N_LAYERS_C, C_AAA_LORA, d), 0.1 * C_AAA_LORA ** -0.5)
    inp['c_k_k'] = 0.85 + nrm((N_LAYERS_C, d), 0.02)
    inp['c_k_a'] = 1.0 + nrm((N_LAYERS_C, d), 0.02)
    inp['c_r_k'] = nrm((N_LAYERS_C, C_HEADS, C_HEAD_DIM), 0.1)
    inp['c_lnx_w'] = 1.0 + nrm((N_LAYERS_C, d), 0.02)
    inp['c_lnx_b'] = nrm((N_LAYERS_C, d), 0.02)
    inp['c_w_out'] = nrm((N_LAYERS_C, d, d), s)
    return inp


def reference(x, meta_tokens, pre_norm_g, post_norm_g,
              a_w_in, a_lb_logits, a_gnorm_g, a_w_out,
              b_w_in, b_lam_q1, b_lam_k1, b_lam_q2, b_lam_k2, b_subln_g, b_w_out,
              c_mu, c_w_in, c_w0, c_w1, c_w2, c_a0, c_a1, c_a2, c_k_k, c_k_a, c_r_k,
              c_lnx_w, c_lnx_b, c_w_out):
    bsz = x.shape[0]
    meta = jnp.broadcast_to(meta_tokens.astype(x.dtype)[None], (bsz, N_META, D_MODEL))
    h = jnp.concatenate([meta, x], axis=1)
    seqlen = h.shape[1]
    pos = jnp.arange(seqlen)
    kcid = chunk_id(pos)
    inv_freq = ROPE_THETA ** (-jnp.arange(0, ROPE_DIM, 2, dtype=F32) / ROPE_DIM)
    ang = pos.astype(F32)[:, None] * inv_freq[None, :]
    cos = jnp.cos(ang)[:, None, None, :]
    sin = jnp.sin(ang)[:, None, None, :]
    lb_p = jax.nn.softmax(a_lb_logits.astype(F32), axis=0)
    lb_all = jnp.cumsum(lb_p, axis=0) - lb_p[0]

    for i in range(DEPTH):
        kind, j = i % N_MIXERS, i // N_MIXERS
        u = rmsnorm(h, pre_norm_g[i])
        if kind == 0:
            y = hgrn2_mixer(u, a_w_in[j], lb_all[j], a_gnorm_g[j], a_w_out[j])
        elif kind == 1:
            lambda_init = 0.8 - 0.6 * math.exp(-0.3 * i)
            y = diff_attn_mixer(u, b_w_in[j], b_lam_q1[j], b_lam_k1[j], b_lam_q2[j], b_lam_k2[j],
                                b_subln_g[j], b_w_out[j], lambda_init, cos, sin, kcid)
        else:
            y = rwkv7_mixer(u, c_mu[j], c_w_in[j], c_w0[j], c_w1[j], c_w2[j], c_a0[j], c_a1[j], c_a2[j],
                            c_k_k[j], c_k_a[j], c_r_k[j], c_lnx_w[j], c_lnx_b[j], c_w_out[j])
        h = h + rmsnorm(y, post_norm_g[i])

    return h[:, N_META:]
```

```python
import functools
import math

import jax
import jax.numpy as jnp
from jax import lax
from jax.experimental import pallas as pl
from jax.experimental.pallas import tpu as pltpu

F32 = jnp.float32
BF16 = jnp.bfloat16

CHUNK = 64
N_META = 16
N_MIXERS = 3
NORM_EPS = 1e-6
A_HEAD_DIM = 128
B_QK_DIM = 128
B_V_DIM = 2 * B_QK_DIM
ROPE_THETA = 500000.0
ROPE_DIM = B_QK_DIM // 4
SUBLN_EPS = 1e-5
C_HEAD_DIM = 64
C_DECAY_SCALE = 0.606531
LNX_EPS = 64e-5

ROW0 = 256
ROW_TILE = 256
SUB = 16
MASK_VALUE = -1e30
V7X_VMEM_LIMIT = 56 * 1024 * 1024

_HI = lax.Precision.HIGHEST


def _cparams(n_axes, vmem=V7X_VMEM_LIMIT):
    return pltpu.CompilerParams(dimension_semantics=("arbitrary",) * n_axes, vmem_limit_bytes=vmem)


def _dot(a, b):
    return jnp.dot(a.astype(BF16), b.astype(BF16), preferred_element_type=F32)


def _dot_nt(a, b):
    return lax.dot_general(a.astype(BF16), b.astype(BF16), (((1,), (1,)), ((), ())), preferred_element_type=F32)


def _dot_tn(a, b):
    return lax.dot_general(a.astype(BF16), b.astype(BF16), (((0,), (0,)), ((), ())), preferred_element_type=F32)


def _dot_f32(a, b):
    return jnp.dot(a, b, preferred_element_type=F32, precision=_HI)


def _iota2(shape, axis):
    return lax.broadcasted_iota(jnp.int32, shape, axis)


def _chunk_cumsum(x):
    tri = (_iota2((CHUNK, CHUNK), 1) <= _iota2((CHUNK, CHUNK), 0)).astype(F32)
    return _dot_f32(tri, x)


def _silu(x):
    return x * jax.nn.sigmoid(x)


def _rms(x, g, eps):
    return x * lax.rsqrt(jnp.mean(x * x, axis=-1, keepdims=True) + eps) * g


def _prenorm_kernel(h_ref, g_ref, u_ref):
    u_ref[...] = _rms(h_ref[...], g_ref[...], NORM_EPS).astype(u_ref.dtype)


def _residual_prenorm_kernel(h_ref, y_ref, gp_ref, g_ref, hn_ref, u_ref):
    hn = h_ref[...] + _rms(y_ref[...], gp_ref[...], NORM_EPS)
    hn_ref[...] = hn
    u_ref[...] = _rms(hn, g_ref[...], NORM_EPS).astype(u_ref.dtype)


def _residual_kernel(h_ref, y_ref, gp_ref, o_ref):
    o_ref[...] = h_ref[...] + _rms(y_ref[...], gp_ref[...], NORM_EPS)


def _prenorm(h, g, u_dtype):
    lp, d = h.shape
    row = pl.BlockSpec((ROW_TILE, d), lambda i: (i, 0))
    vec = pl.BlockSpec((1, d), lambda i: (0, 0))
    return pl.pallas_call(
        _prenorm_kernel, grid=(lp // ROW_TILE,), in_specs=[row, vec], out_specs=row,
        out_shape=jax.ShapeDtypeStruct((lp, d), u_dtype), compiler_params=_cparams(1), name="prenorm",
    )(h, g.reshape(1, d))


def _residual_prenorm(h, y, gp, g, u_dtype):
    lp, d = h.shape
    row = pl.BlockSpec((ROW_TILE, d), lambda i: (i, 0))
    vec = pl.BlockSpec((1, d), lambda i: (0, 0))
    return pl.pallas_call(
        _residual_prenorm_kernel, grid=(lp // ROW_TILE,), in_specs=[row, row, vec, vec], out_specs=[row, row],
        out_shape=[jax.ShapeDtypeStruct((lp, d), F32), jax.ShapeDtypeStruct((lp, d), u_dtype)],
        compiler_params=_cparams(1), name="residual_prenorm",
    )(h, y, gp.reshape(1, d), g.reshape(1, d))


def _residual_out(h, y, gp, seq):
    lp, d = h.shape
    off = ROW0 // ROW_TILE
    row_in = pl.BlockSpec((ROW_TILE, d), lambda i: (i + off, 0))
    vec = pl.BlockSpec((1, d), lambda i: (0, 0))
    return pl.pallas_call(
        _residual_kernel, grid=(seq // ROW_TILE,), in_specs=[row_in, row_in, vec],
        out_specs=pl.BlockSpec((ROW_TILE, d), lambda i: (i, 0)),
        out_shape=jax.ShapeDtypeStruct((seq, d), F32), compiler_params=_cparams(1), name="residual_out",
    )(h, y, gp.reshape(1, d))


def _mm_kernel(a_ref, w_ref, o_ref):
    o_ref[...] = jnp.dot(a_ref[...], w_ref[...], preferred_element_type=F32).astype(o_ref.dtype)


def _row_block(m, cap=1280):
    n = m // ROW_TILE
    best = max(d for d in range(1, n + 1) if n % d == 0 and ROW_TILE * d <= cap)
    return ROW_TILE * best


def _lhs_spec(a, tm, lead, index_map):
    k = a.shape[-1]
    if a.ndim == 2:
        return pl.BlockSpec((tm, k), lambda *g: (index_map(*g), 0))
    return pl.BlockSpec((None, tm, k), lambda *g: (lead, index_map(*g), 0))


def _matmul(a, w, n_out, col_off=0, lead=0, out_dtype=F32, tn=512):
    m = a.shape[-2]
    k = a.shape[-1]
    tm = _row_block(m)
    tn = min(tn, n_out)
    joff = col_off // tn
    return pl.pallas_call(
        _mm_kernel, grid=(m // tm, n_out // tn),
        in_specs=[_lhs_spec(a, tm, lead, lambda i, j: i), pl.BlockSpec((k, tn), lambda i, j: (0, j + joff))],
        out_specs=pl.BlockSpec((tm, tn), lambda i, j: (i, j)),
        out_shape=jax.ShapeDtypeStruct((m, n_out), out_dtype), compiler_params=_cparams(2), name="matmul",
    )(a, w)


def _lora_kernel(x_ref, a_ref, b_ref, o_ref, *, use_tanh):
    mid = jnp.dot(x_ref[...], a_ref[...], preferred_element_type=F32)
    if use_tanh:
        mid = jnp.tanh(mid)
    o_ref[...] = _dot(mid, b_ref[...])


def _lora(x, lead, a, b, use_tanh):
    m, d = x.shape[-2:]
    r = a.shape[1]
    return pl.pallas_call(
        functools.partial(_lora_kernel, use_tanh=use_tanh), grid=(m // ROW_TILE,),
        in_specs=[_lhs_spec(x, ROW_TILE, lead, lambda i: i), pl.BlockSpec((d, r), lambda i: (0, 0)),
                  pl.BlockSpec((r, d), lambda i: (0, 0))],
        out_specs=pl.BlockSpec((ROW_TILE, d), lambda i: (i, 0)),
        out_shape=jax.ShapeDtypeStruct((m, d), F32), compiler_params=_cparams(1), name="lora",
    )(x, a, b)


def _hgrn2_chunk(q, fl, v, lb, st):
    f = lb + (1.0 - lb) * jax.nn.sigmoid(fl)
    g = jnp.log(f)
    k = 1.0 - f
    b = _chunk_cumsum(g)
    b_last = b[CHUNK - 1:CHUNK, :]
    out = _dot_nt(q * jnp.exp(b), st)
    st_new = st * jnp.exp(b_last) + _dot_tn(v, k * jnp.exp(b_last - b))

    n_sub = CHUNK // SUB
    col = _iota2((SUB, CHUNK), 1)
    blocks = [jnp.zeros((SUB, CHUNK), F32)]
    for i in range(1, n_sub):
        b_ref = b[SUB * i:SUB * i + 1, :]
        q_i = q[SUB * i:SUB * (i + 1)] * jnp.exp(b[SUB * i:SUB * (i + 1)] - b_ref)
        k_i = k * jnp.exp(jnp.minimum(b_ref - b, 0.0))
        blocks.append(jnp.where(col < SUB * i, _dot_nt(q_i, k_i), 0.0))
    out = out + _dot(jnp.concatenate(blocks, axis=0), v)

    ones = jnp.ones((A_HEAD_DIM, A_HEAD_DIM), BF16)
    row = _iota2((SUB, A_HEAD_DIM), 0)
    diag = []
    for i in range(n_sub):
        qb, kb, bb, vb = (t[SUB * i:SUB * (i + 1)] for t in (q, k, b, v))
        terms = []
        for s in range(SUB):
            e = jnp.exp(jnp.minimum(bb - bb[s:s + 1, :], 0.0))
            terms.append(jnp.where(row >= s, qb * e, 0.0) * kb[s:s + 1, :])
        sums = jnp.dot(jnp.concatenate(terms, axis=0).astype(BF16), ones, preferred_element_type=F32)
        acc = jnp.zeros((SUB, A_HEAD_DIM), F32)
        for s in range(SUB):
            acc = acc + sums[SUB * s:SUB * (s + 1)] * vb[s:s + 1, :]
        diag.append(acc)
    return out + jnp.concatenate(diag, axis=0), st_new


def _hgrn2_kernel(q_ref, f_ref, i_ref, g_ref, lb_ref, gn_ref, o_ref, st_ref, *, heads, chunks):
    @pl.when(pl.program_id(1) == 0)
    def _():
        st_ref[...] = jnp.zeros_like(st_ref)

    def body(c, carry):
        rows = pl.ds(pl.multiple_of(c * CHUNK, CHUNK), CHUNK)
        for h in range(heads):
            cols = slice(h * A_HEAD_DIM, (h + 1) * A_HEAD_DIM)
            o, st_new = _hgrn2_chunk(q_ref[rows, cols], f_ref[rows, cols], i_ref[rows, cols],
                                     lb_ref[:, cols], st_ref[h])
            st_ref[h] = st_new
            o = _rms(o, gn_ref[...], NORM_EPS)
            o_ref[rows, cols] = (o * _silu(g_ref[rows, cols])).astype(o_ref.dtype)
        return carry

    lax.fori_loop(0, chunks, body, 0)


def _hgrn2_core(p, lb, gnorm_g, d):
    lp = p.shape[0]
    heads = 4
    width = heads * A_HEAD_DIM
    ncol = d // width
    blk = lambda part: pl.BlockSpec((ROW_TILE, width), lambda j, r: (r, part * ncol + j))
    return pl.pallas_call(
        functools.partial(_hgrn2_kernel, heads=heads, chunks=ROW_TILE // CHUNK),
        grid=(ncol, lp // ROW_TILE),
        in_specs=[blk(0), blk(1), blk(2), blk(3), pl.BlockSpec((1, width), lambda j, r: (0, j)),
                  pl.BlockSpec((1, A_HEAD_DIM), lambda j, r: (0, 0))],
        out_specs=pl.BlockSpec((ROW_TILE, width), lambda j, r: (r, j)),
        out_shape=jax.ShapeDtypeStruct((lp, d), BF16),
        scratch_shapes=[pltpu.VMEM((heads, A_HEAD_DIM, A_HEAD_DIM), F32)],
        compiler_params=_cparams(2), name="hgrn2",
    )(p, p, p, p, lb.reshape(1, d), gnorm_g.reshape(1, A_HEAD_DIM))


def _rope_kernel(p_ref, cos_ref, sin_ref, o_ref, *, d):
    part = pl.program_id(1)

    @pl.when(part < 2)
    def _():
        lane = _iota2((ROW_TILE, B_QK_DIM), 1)
        first = lane < ROPE_DIM // 2
        cos = cos_ref[...]
        sin = sin_ref[...]
        for c in range(d // B_QK_DIM):
            cols = slice(c * B_QK_DIM, (c + 1) * B_QK_DIM)
            x = p_ref[:, cols]
            other = jnp.where(first, pltpu.roll(x, B_QK_DIM - ROPE_DIM // 2, 1), pltpu.roll(x, ROPE_DIM // 2, 1))
            o_ref[:, cols] = (x * cos + other * sin).astype(o_ref.dtype)

    @pl.when(part == 2)
    def _():
        o_ref[...] = p_ref[...].astype(o_ref.dtype)


def _rope_qkv(p, cos_t, sin_t, d):
    lp = p.shape[0]
    return pl.pallas_call(
        functools.partial(_rope_kernel, d=d), grid=(lp // ROW_TILE, 3),
        in_specs=[pl.BlockSpec((ROW_TILE, d), lambda r, c: (r, c)),
                  pl.BlockSpec((ROW_TILE, B_QK_DIM), lambda r, c: (r, 0)),
                  pl.BlockSpec((ROW_TILE, B_QK_DIM), lambda r, c: (r, 0))],
        out_specs=pl.BlockSpec((ROW_TILE, d), lambda r, c: (r, c)),
        out_shape=jax.ShapeDtypeStruct((lp, 3 * d), BF16), compiler_params=_cparams(2), name="rope_qkv",
    )(p, cos_t, sin_t)


def _row_chunk_id(row):
    return jnp.maximum((row - (ROW0 - CHUNK)) >> 6, 0)


def _attn_kernel(q_ref, k_ref, v_ref, g_ref, lam_ref, sg_ref, o_ref, acc_ref, m_ref, l_ref, *, tq, lambda_init):
    i = pl.program_id(1)
    scale = B_QK_DIM ** -0.5
    acc_ref[...] = jnp.zeros_like(acc_ref)
    m_ref[...] = jnp.full_like(m_ref, MASK_VALUE)
    l_ref[...] = jnp.zeros_like(l_ref)
    q = q_ref[...]

    def step(j, masked):
        rows = pl.ds(pl.multiple_of(j * tq, tq), tq)
        kj = k_ref[rows, :]
        vj = v_ref[rows, :]
        if masked:
            q_row = i * tq + _iota2((tq, tq), 0)
            k_row = j * tq + _iota2((tq, tq), 1)
            visible = (_row_chunk_id(k_row) <= _row_chunk_id(q_row)) & (k_row >= ROW0 - N_META)
        for c in range(2):
            cols = slice(c * B_QK_DIM, (c + 1) * B_QK_DIM)
            s = lax.dot_general(q[:, cols], kj[:, cols], (((1,), (1,)), ((), ())), preferred_element_type=F32) * scale
            if masked:
                s = jnp.where(visible, s, MASK_VALUE)
            m_prev = m_ref[c]
            m_new = jnp.maximum(m_prev, jnp.max(s, axis=1, keepdims=True))
            alpha = jnp.exp(m_prev - m_new)
            p = jnp.exp(s - pltpu.repeat(m_new, tq // 128, axis=1))
            l_ref[c] = alpha * l_ref[c] + jnp.sum(p, axis=1, keepdims=True)
            m_ref[c] = m_new
            acc_ref[c] = acc_ref[c] * pltpu.repeat(alpha, B_V_DIM // 128, axis=1) + jnp.dot(
                p.astype(BF16), vj, preferred_element_type=F32)

    step(0, True)

    def full_tile(j, carry):
        step(j, False)
        return carry

    lax.fori_loop(1, i, full_tile, 0)

    @pl.when(i > 0)
    def _():
        step(i, True)

    lam4 = lam_ref[...]
    lam = (jnp.exp(jnp.sum(lam4[0:1] * lam4[1:2], axis=1, keepdims=True))
           - jnp.exp(jnp.sum(lam4[2:3] * lam4[3:4], axis=1, keepdims=True)) + lambda_init)
    rep = B_V_DIM // 128
    o = (acc_ref[0] / pltpu.repeat(l_ref[0], rep, axis=1)
         - lam * (acc_ref[1] / pltpu.repeat(l_ref[1], rep, axis=1)))
    o = _rms(o, sg_ref[...], SUBLN_EPS) * (1.0 - lambda_init)
    o_ref[...] = (o * _silu(g_ref[...])).astype(o_ref.dtype)


def _attn_core(qkv, p, lam4, subln_g, d, lambda_init):
    lp = qkv.shape[0]
    heads = d // B_V_DIM
    tq = ROW_TILE
    kernel = functools.partial(_attn_kernel, tq=tq, lambda_init=lambda_init)
    return pl.pallas_call(
        kernel, grid=(heads, lp // tq),
        in_specs=[pl.BlockSpec((tq, B_V_DIM), lambda h, i: (i, h)),
                  pl.BlockSpec((lp, B_V_DIM), lambda h, i: (0, heads + h)),
                  pl.BlockSpec((lp, B_V_DIM), lambda h, i: (0, 2 * heads + h)),
                  pl.BlockSpec((tq, B_V_DIM), lambda h, i: (i, 3 * heads + h)),
                  pl.BlockSpec((4, B_QK_DIM), lambda h, i: (0, 0)),
                  pl.BlockSpec((1, B_V_DIM), lambda h, i: (0, 0))],
        out_specs=pl.BlockSpec((tq, B_V_DIM), lambda h, i: (i, h)),
        out_shape=jax.ShapeDtypeStruct((lp, d), BF16),
        scratch_shapes=[pltpu.VMEM((2, tq, B_V_DIM), F32), pltpu.VMEM((2, tq, 128), F32),
                        pltpu.VMEM((2, tq, 128), F32)],
        compiler_params=_cparams(2), name="diff_attn",
    )(qkv, qkv, qkv, p, lam4, subln_g.reshape(1, B_V_DIM))


def _rope_tables(lp):
    pos = (jnp.arange(lp) - (ROW0 - N_META)).astype(F32)
    inv_freq = ROPE_THETA ** (-jnp.arange(0, ROPE_DIM, 2, dtype=F32) / ROPE_DIM)
    ang = pos[:, None] * inv_freq[None, :]
    cos, sin = jnp.cos(ang), jnp.sin(ang)
    pad = B_QK_DIM - ROPE_DIM
    cos_t = jnp.concatenate([cos, cos, jnp.ones((lp, pad), F32)], axis=1)
    sin_t = jnp.concatenate([-sin, sin, jnp.zeros((lp, pad), F32)], axis=1)
    return cos_t, sin_t


def _shift_mix_kernel(u_ref, prev_ref, mu_ref, o_ref):
    u = u_ref[...]
    last = prev_ref[7:8, :]
    last = jnp.where(pl.program_id(0) == 0, jnp.zeros_like(last), last)
    row = _iota2(u.shape, 0)
    prev = jnp.where(row == 0, last, pltpu.roll(u, 1, 0))
    delta = prev - u
    for s in range(o_ref.shape[0]):
        o_ref[s] = (u + delta * mu_ref[s:s + 1, :]).astype(o_ref.dtype)


def _shift_mix(u, mu):
    lp, d = u.shape
    n = mu.shape[0]
    per8 = ROW_TILE // 8
    return pl.pallas_call(
        _shift_mix_kernel, grid=(lp // ROW_TILE,),
        in_specs=[pl.BlockSpec((ROW_TILE, d), lambda i: (i, 0)),
                  pl.BlockSpec((8, d), lambda i: (jnp.maximum(i * per8 - 1, 0), 0)),
                  pl.BlockSpec((n, d), lambda i: (0, 0))],
        out_specs=pl.BlockSpec((n, ROW_TILE, d), lambda i: (0, i, 0)),
        out_shape=jax.ShapeDtypeStruct((n, lp, d), BF16), compiler_params=_cparams(1), name="shift_mix",
    )(u, u, mu)


def _rwkv_kernel(r_ref, k_ref, v_ref, g_ref, dw_ref, da_ref, w0_ref, a0_ref, kk_ref, ka_ref, rk_ref,
                 lw_ref, lb_ref, o_ref, st_ref, *, heads):
    @pl.when(pl.program_id(1) == 0)
    def _():
        st_ref[...] = jnp.zeros_like(st_ref)

    n = C_HEAD_DIM
    width = heads * n
    r = r_ref[...]
    k = k_ref[...]
    v = v_ref[...]
    shift = int(math.log2(n))
    seg = ((_iota2((width, width), 0) >> shift) == (_iota2((width, width), 1) >> shift)).astype(BF16)

    def seg_sum(x):
        return jnp.dot(x.astype(BF16), seg, preferred_element_type=F32)

    logw = -C_DECAY_SCALE * jax.nn.sigmoid(w0_ref[...] + dw_ref[...])
    alpha = jax.nn.sigmoid(a0_ref[...] + da_ref[...])
    kk = k * kk_ref[...]
    kk = kk / jnp.maximum(jnp.sqrt(seg_sum(kk * kk)), 1e-12)
    k2 = k * (1.0 + (alpha - 1.0) * ka_ref[...])
    cum = _chunk_cumsum(logw)
    cum_last = cum[CHUNK - 1:CHUNK, :]
    b = kk * alpha
    e_inv = jnp.exp(-cum)
    e_tail = jnp.exp(cum_last - cum)
    p_t = -kk * jnp.exp(cum - logw)
    r_t = r * jnp.exp(cum)
    b_t = b * e_inv
    k_t = k2 * e_inv
    b_h = b * e_tail
    k_h = k2 * e_tail
    gam = jnp.exp(cum_last)

    ri = _iota2((CHUNK, CHUNK), 0)
    ci = _iota2((CHUNK, CHUNK), 1)
    strict = ci < ri
    lower = ci <= ri
    eye = ci == ri

    outs = []
    for h in range(heads):
        sl = slice(h * n, (h + 1) * n)
        a_all = _dot_nt(jnp.concatenate([p_t[:, sl], r_t[:, sl]], axis=0),
                        jnp.concatenate([b_t[:, sl], k_t[:, sl]], axis=0))
        l1 = jnp.where(strict, a_all[:CHUNK, :CHUNK], 0.0)
        l2 = jnp.where(strict, a_all[:CHUNK, CHUNK:], 0.0)
        a3 = jnp.where(lower, a_all[CHUNK:, :CHUNK], 0.0)
        a4 = jnp.where(lower, a_all[CHUNK:, CHUNK:], 0.0)
        power = l1
        t_inv = jnp.where(eye, 1.0, l1)
        for _ in range(int(math.log2(CHUNK)) - 1):
            power = _dot_f32(power, power)
            t_inv = t_inv + _dot_f32(t_inv, power)
        vh = v[:, sl]
        wv = _dot_f32(t_inv, jnp.concatenate([p_t[:, sl], _dot(l2, vh)], axis=1))
        qo = _dot(a3, wv)
        q_t = r_t[:, sl] + qo[:, :n]
        o_intra = qo[:, n:] + _dot(a4, vh)
        mz = _dot_tn(b_h[:, sl], wv)
        m_mat = mz[:, :n] + jnp.where(eye, gam[:, sl], 0.0)
        z_mat = mz[:, n:] + _dot_tn(k_h[:, sl], vh)
        st = st_ref[h]
        outs.append(_dot(q_t, st) + o_intra)
        st_ref[h] = _dot(m_mat, st) + z_mat
    o = jnp.concatenate(outs, axis=1)

    mean = seg_sum(o) * (1.0 / n)
    cen = o - mean
    var = seg_sum(cen * cen) * (1.0 / n)
    o = cen * lax.rsqrt(var + LNX_EPS) * lw_ref[...] + lb_ref[...]
    o = o + seg_sum(r * k2 * rk_ref[...]) * v
    o_ref[...] = (o * _silu(g_ref[...])).astype(o_ref.dtype)


def _rwkv_core(r, k, v, gate, dw, da, vecs, d):
    lp = r.shape[0]
    heads = 8
    width = heads * C_HEAD_DIM
    act = pl.BlockSpec((CHUNK, width), lambda j, c: (c, j))
    vec = pl.BlockSpec((1, width), lambda j, c: (0, j))
    return pl.pallas_call(
        functools.partial(_rwkv_kernel, heads=heads), grid=(d // width, lp // CHUNK),
        in_specs=[act] * 6 + [vec] * 7, out_specs=act,
        out_shape=jax.ShapeDtypeStruct((lp, d), BF16),
        scratch_shapes=[pltpu.VMEM((heads, C_HEAD_DIM, C_HEAD_DIM), F32)],
        compiler_params=_cparams(2), name="rwkv7",
    )(r, k, v, gate, dw, da, *[t.reshape(1, d) for t in vecs])


def _hgrn2_layer(u, w_in, lb, gnorm_g, w_out, d):
    p = _matmul(u, w_in.astype(BF16), 4 * d)
    return _matmul(_hgrn2_core(p, lb, gnorm_g, d), w_out.astype(BF16), d)


def _attn_layer(u, w_in, lam4, subln_g, w_out, d, lambda_init, tables):
    p = _matmul(u, w_in.astype(BF16), 4 * d)
    qkv = _rope_qkv(p, *tables, d)
    return _matmul(_attn_core(qkv, p, lam4, subln_g, d, lambda_init), w_out.astype(BF16), d)


def _rwkv_layer(u, mu, w_in, w0, w1, w2, a0, a1, a2, k_k, k_a, r_k, lnx_w, lnx_b, w_out, d):
    xs = _shift_mix(u, mu)
    w_in = w_in.astype(BF16)
    r, k, v, gate = (_matmul(xs, w_in, d, col_off=s * d, lead=s) for s in range(4))
    dw = _lora(xs, 4, w1.astype(BF16), w2.astype(BF16), True)
    da = _lora(xs, 5, a1.astype(BF16), a2.astype(BF16), False)
    o = _rwkv_core(r, k, v, gate, dw, da, (w0, a0, k_k, k_a, r_k.reshape(-1), lnx_w, lnx_b), d)
    return _matmul(o, w_out.astype(BF16), d)


def kernel(x, meta_tokens, pre_norm_g, post_norm_g, a_w_in, a_lb_logits, a_gnorm_g, a_w_out, b_w_in, b_lam_q1, b_lam_k1, b_lam_q2, b_lam_k2, b_subln_g, b_w_out, c_mu, c_w_in, c_w0, c_w1, c_w2, c_a0, c_a1, c_a2, c_k_k, c_k_a, c_r_k, c_lnx_w, c_lnx_b, c_w_out):
    bsz, seq, d = x.shape
    depth = pre_norm_g.shape[0]
    assert bsz == 1 and seq % ROW_TILE == 0 and meta_tokens.shape[0] == N_META
    lp = ROW0 + seq
    h = jnp.concatenate([jnp.zeros((ROW0 - N_META, d), F32), meta_tokens.astype(F32), x[0]], axis=0)
    tables = _rope_tables(lp)
    lb_p = jax.nn.softmax(a_lb_logits.astype(F32), axis=0)
    lb_all = jnp.cumsum(lb_p, axis=0) - lb_p[0]

    def u_dtype(i):
        return F32 if i % N_MIXERS == 2 else BF16

    u = _prenorm(h, pre_norm_g[0], u_dtype(0))
    for i in range(depth):
        kind, j = i % N_MIXERS, i // N_MIXERS
        if kind == 0:
            y = _hgrn2_layer(u, a_w_in[j], lb_all[j], a_gnorm_g[j], a_w_out[j], d)
        elif kind == 1:
            lambda_init = 0.8 - 0.6 * math.exp(-0.3 * i)
            lam4 = jnp.stack([b_lam_q1[j], b_lam_k1[j], b_lam_q2[j], b_lam_k2[j]]).astype(F32)
            y = _attn_layer(u, b_w_in[j], lam4, b_subln_g[j], b_w_out[j], d, lambda_init, tables)
        else:
            y = _rwkv_layer(u, c_mu[j], c_w_in[j], c_w0[j], c_w1[j], c_w2[j], c_a0[j], c_a1[j], c_a2[j],
                            c_k_k[j], c_k_a[j], c_r_k[j], c_lnx_w[j], c_lnx_b[j], c_w_out[j], d)
        if i + 1 < depth:
            h, u = _residual_prenorm(h, y, post_norm_g[i], pre_norm_g[i + 1], u_dtype(i + 1))
        else:
            out = _residual_out(h, y, post_norm_g[i], seq)
    return out[None]
```

```python
import functools
import math

import jax
import jax.numpy as jnp
from jax import lax
from jax.experimental import pallas as pl
from jax.experimental.pallas import tpu as pltpu

F32 = jnp.float32
BF16 = jnp.bfloat16

CHUNK = 64
N_META = 16
N_MIXERS = 3
NORM_EPS = 1e-6
A_HEAD_DIM = 128
B_QK_DIM = 128
B_V_DIM = 2 * B_QK_DIM
ROPE_THETA = 500000.0
ROPE_DIM = B_QK_DIM // 4
SUBLN_EPS = 1e-5
C_HEAD_DIM = 64
C_DECAY_SCALE = 0.606531
LNX_EPS = 64e-5

ROW0 = 256
ROW_TILE = 256
SUB = 16
MASK_VALUE = -1e30
Q_SCALE_LOG2 = B_QK_DIM ** -0.5 * math.log2(math.e)
V7X_VMEM_LIMIT = 56 * 1024 * 1024


def _cparams(n_axes, vmem=V7X_VMEM_LIMIT):
    return pltpu.CompilerParams(dimension_semantics=("arbitrary",) * n_axes, vmem_limit_bytes=vmem)


def _dot(a, b):
    return jnp.dot(a.astype(BF16), b.astype(BF16), preferred_element_type=F32)


def _dot_nt(a, b):
    return lax.dot_general(a.astype(BF16), b.astype(BF16), (((1,), (1,)), ((), ())), preferred_element_type=F32)


def _dot_tn(a, b):
    return lax.dot_general(a.astype(BF16), b.astype(BF16), (((0,), (0,)), ((), ())), preferred_element_type=F32)


def _iota2(shape, axis):
    return lax.broadcasted_iota(jnp.int32, shape, axis)


def _chunk_cumsum(x):
    tri = (_iota2((CHUNK, CHUNK), 1) <= _iota2((CHUNK, CHUNK), 0)).astype(BF16)
    hi = x.astype(BF16)
    lo = (x - hi.astype(F32)).astype(BF16)
    return jnp.dot(tri, hi, preferred_element_type=F32) + jnp.dot(tri, lo, preferred_element_type=F32)


def _silu(x):
    return x * jax.nn.sigmoid(x)


def _rms(x, g, eps):
    return x * lax.rsqrt(jnp.mean(x * x, axis=-1, keepdims=True) + eps) * g


def _prenorm_kernel(h_ref, g_ref, u_ref):
    u_ref[...] = _rms(h_ref[...], g_ref[...], NORM_EPS).astype(u_ref.dtype)


def _residual_prenorm_kernel(h_ref, y_ref, gp_ref, g_ref, hn_ref, u_ref):
    hn = h_ref[...] + _rms(y_ref[...], gp_ref[...], NORM_EPS)
    hn_ref[...] = hn
    u_ref[...] = _rms(hn, g_ref[...], NORM_EPS).astype(u_ref.dtype)


def _residual_kernel(h_ref, y_ref, gp_ref, o_ref):
    o_ref[...] = h_ref[...] + _rms(y_ref[...], gp_ref[...], NORM_EPS)


def _prenorm(h, g, u_dtype):
    lp, d = h.shape
    row = pl.BlockSpec((ROW_TILE, d), lambda i: (i, 0))
    vec = pl.BlockSpec((1, d), lambda i: (0, 0))
    return pl.pallas_call(
        _prenorm_kernel, grid=(lp // ROW_TILE,), in_specs=[row, vec], out_specs=row,
        out_shape=jax.ShapeDtypeStruct((lp, d), u_dtype), compiler_params=_cparams(1), name="prenorm",
    )(h, g.reshape(1, d))


def _residual_prenorm(h, y, gp, g, u_dtype):
    lp, d = h.shape
    row = pl.BlockSpec((ROW_TILE, d), lambda i: (i, 0))
    vec = pl.BlockSpec((1, d), lambda i: (0, 0))
    return pl.pallas_call(
        _residual_prenorm_kernel, grid=(lp // ROW_TILE,), in_specs=[row, row, vec, vec], out_specs=[row, row],
        out_shape=[jax.ShapeDtypeStruct((lp, d), F32), jax.ShapeDtypeStruct((lp, d), u_dtype)],
        compiler_params=_cparams(1), name="residual_prenorm",
    )(h, y, gp.reshape(1, d), g.reshape(1, d))


def _residual_out(h, y, gp, seq):
    lp, d = h.shape
    off = ROW0 // ROW_TILE
    row_in = pl.BlockSpec((ROW_TILE, d), lambda i: (i + off, 0))
    vec = pl.BlockSpec((1, d), lambda i: (0, 0))
    return pl.pallas_call(
        _residual_kernel, grid=(seq // ROW_TILE,), in_specs=[row_in, row_in, vec],
        out_specs=pl.BlockSpec((ROW_TILE, d), lambda i: (i, 0)),
        out_shape=jax.ShapeDtypeStruct((seq, d), F32), compiler_params=_cparams(1), name="residual_out",
    )(h, y, gp.reshape(1, d))


def _mm_kernel(a_ref, w_ref, o_ref):
    o_ref[...] = jnp.dot(a_ref[...], w_ref[...], preferred_element_type=F32).astype(o_ref.dtype)


def _row_block(m, cap=1280):
    n = m // ROW_TILE
    best = max(d for d in range(1, n + 1) if n % d == 0 and ROW_TILE * d <= cap)
    return ROW_TILE * best


def _lhs_spec(a, tm, lead, index_map):
    k = a.shape[-1]
    if a.ndim == 2:
        return pl.BlockSpec((tm, k), lambda *g: (index_map(*g), 0))
    return pl.BlockSpec((None, tm, k), lambda *g: (lead, index_map(*g), 0))


def _matmul(a, w, n_out, col_off=0, lead=0, out_dtype=F32, tn=512):
    m = a.shape[-2]
    k = a.shape[-1]
    tm = _row_block(m)
    tn = min(tn, n_out)
    joff = col_off // tn
    return pl.pallas_call(
        _mm_kernel, grid=(m // tm, n_out // tn),
        in_specs=[_lhs_spec(a, tm, lead, lambda i, j: i), pl.BlockSpec((k, tn), lambda i, j: (0, j + joff))],
        out_specs=pl.BlockSpec((tm, tn), lambda i, j: (i, j)),
        out_shape=jax.ShapeDtypeStruct((m, n_out), out_dtype), compiler_params=_cparams(2), name="matmul",
    )(a, w)


def _lora_kernel(x_ref, a_ref, b_ref, o_ref, *, use_tanh):
    mid = jnp.dot(x_ref[...], a_ref[...], preferred_element_type=F32)
    if use_tanh:
        mid = jnp.tanh(mid)
    o_ref[...] = _dot(mid, b_ref[...])


def _lora(x, lead, a, b, use_tanh):
    m, d = x.shape[-2:]
    r = a.shape[1]
    return pl.pallas_call(
        functools.partial(_lora_kernel, use_tanh=use_tanh), grid=(m // ROW_TILE,),
        in_specs=[_lhs_spec(x, ROW_TILE, lead, lambda i: i), pl.BlockSpec((d, r), lambda i: (0, 0)),
                  pl.BlockSpec((r, d), lambda i: (0, 0))],
        out_specs=pl.BlockSpec((ROW_TILE, d), lambda i: (i, 0)),
        out_shape=jax.ShapeDtypeStruct((m, d), F32), compiler_params=_cparams(1), name="lora",
    )(x, a, b)


def _hgrn2_chunk(q, fl, v, lb, st):
    f = lb + (1.0 - lb) * jax.nn.sigmoid(fl)
    g = jnp.log(f)
    k = 1.0 - f
    b = _chunk_cumsum(g)
    b_last = b[CHUNK - 1:CHUNK, :]
    out = _dot_nt(q * jnp.exp(b), st)
    st_new = st * jnp.exp(b_last) + _dot_tn(v, k * jnp.exp(b_last - b))

    n_sub = CHUNK // SUB
    col = _iota2((SUB, CHUNK), 1)
    blocks = [jnp.zeros((SUB, CHUNK), F32)]
    for i in range(1, n_sub):
        b_ref = b[SUB * i:SUB * i + 1, :]
        q_i = q[SUB * i:SUB * (i + 1)] * jnp.exp(b[SUB * i:SUB * (i + 1)] - b_ref)
        k_i = k * jnp.exp(jnp.minimum(b_ref - b, 0.0))
        blocks.append(jnp.where(col < SUB * i, _dot_nt(q_i, k_i), 0.0))
    out = out + _dot(jnp.concatenate(blocks, axis=0), v)

    ones = jnp.ones((A_HEAD_DIM, A_HEAD_DIM), BF16)
    row = _iota2((SUB, A_HEAD_DIM), 0)
    diag = []
    for i in range(n_sub):
        qb, kb, bb, vb = (t[SUB * i:SUB * (i + 1)] for t in (q, k, b, v))
        terms = []
        for s in range(SUB):
            e = jnp.exp(jnp.minimum(bb - bb[s:s + 1, :], 0.0))
            terms.append(jnp.where(row >= s, qb * e, 0.0) * kb[s:s + 1, :])
        sums = jnp.dot(jnp.concatenate(terms, axis=0).astype(BF16), ones, preferred_element_type=F32)
        acc = jnp.zeros((SUB, A_HEAD_DIM), F32)
        for s in range(SUB):
            acc = acc + sums[SUB * s:SUB * (s + 1)] * vb[s:s + 1, :]
        diag.append(acc)
    return out + jnp.concatenate(diag, axis=0), st_new


def _hgrn2_kernel(q_ref, f_ref, i_ref, g_ref, lb_ref, gn_ref, o_ref, st_ref, *, heads, chunks):
    @pl.when(pl.program_id(1) == 0)
    def _():
        st_ref[...] = jnp.zeros_like(st_ref)

    def body(c, carry):
        rows = pl.ds(pl.multiple_of(c * CHUNK, CHUNK), CHUNK)
        for h in range(heads):
            cols = slice(h * A_HEAD_DIM, (h + 1) * A_HEAD_DIM)
            o, st_new = _hgrn2_chunk(q_ref[rows, cols], f_ref[rows, cols], i_ref[rows, cols],
                                     lb_ref[:, cols], st_ref[h])
            st_ref[h] = st_new
            o = _rms(o, gn_ref[...], NORM_EPS)
            o_ref[rows, cols] = (o * _silu(g_ref[rows, cols])).astype(o_ref.dtype)
        return carry

    lax.fori_loop(0, chunks, body, 0)


def _hgrn2_core(p, lb, gnorm_g, d):
    lp = p.shape[0]
    heads = 4
    width = heads * A_HEAD_DIM
    ncol = d // width
    blk = lambda part: pl.BlockSpec((ROW_TILE, width), lambda j, r: (r, part * ncol + j))
    return pl.pallas_call(
        functools.partial(_hgrn2_kernel, heads=heads, chunks=ROW_TILE // CHUNK),
        grid=(ncol, lp // ROW_TILE),
        in_specs=[blk(0), blk(1), blk(2), blk(3), pl.BlockSpec((1, width), lambda j, r: (0, j)),
                  pl.BlockSpec((1, A_HEAD_DIM), lambda j, r: (0, 0))],
        out_specs=pl.BlockSpec((ROW_TILE, width), lambda j, r: (r, j)),
        out_shape=jax.ShapeDtypeStruct((lp, d), BF16),
        scratch_shapes=[pltpu.VMEM((heads, A_HEAD_DIM, A_HEAD_DIM), F32)],
        compiler_params=_cparams(2), name="hgrn2",
    )(p, p, p, p, lb.reshape(1, d), gnorm_g.reshape(1, A_HEAD_DIM))


def _rope_kernel(p_ref, cos_ref, sin_ref, o_ref, *, d):
    part = pl.program_id(1)

    @pl.when(part < 2)
    def _():
        lane = _iota2((ROW_TILE, B_QK_DIM), 1)
        first = lane < ROPE_DIM // 2
        fac = jnp.where(part == 0, Q_SCALE_LOG2, 1.0)
        cos = cos_ref[...] * fac
        sin = sin_ref[...] * fac
        for c in range(d // B_QK_DIM):
            cols = slice(c * B_QK_DIM, (c + 1) * B_QK_DIM)
            x = p_ref[:, cols]
            other = jnp.where(first, pltpu.roll(x, B_QK_DIM - ROPE_DIM // 2, 1), pltpu.roll(x, ROPE_DIM // 2, 1))
            o_ref[:, cols] = (x * cos + other * sin).astype(o_ref.dtype)

    @pl.when(part == 2)
    def _():
        o_ref[...] = p_ref[...].astype(o_ref.dtype)


def _rope_qkv(p, cos_t, sin_t, d):
    lp = p.shape[0]
    return pl.pallas_call(
        functools.partial(_rope_kernel, d=d), grid=(lp // ROW_TILE, 3),
        in_specs=[pl.BlockSpec((ROW_TILE, d), lambda r, c: (r, c)),
                  pl.BlockSpec((ROW_TILE, B_QK_DIM), lambda r, c: (r, 0)),
                  pl.BlockSpec((ROW_TILE, B_QK_DIM), lambda r, c: (r, 0))],
        out_specs=pl.BlockSpec((ROW_TILE, d), lambda r, c: (r, c)),
        out_shape=jax.ShapeDtypeStruct((lp, 3 * d), BF16), compiler_params=_cparams(2), name="rope_qkv",
    )(p, cos_t, sin_t)


def _row_chunk_id(row):
    return jnp.maximum((row - (ROW0 - CHUNK)) >> 6, 0)


def _attn_kernel(q_ref, k_ref, v_ref, g_ref, lam_ref, sg_ref, o_ref, qx_ref, acc_ref, m_ref, l_ref, *,
                 tq, lp, lambda_init):
    i = pl.program_id(1)
    q = q_ref[...]
    lane = _iota2(q.shape, 1)
    qx_ref[0:tq, :] = jnp.where(lane < B_QK_DIM, q, jnp.zeros_like(q))
    qx_ref[tq:2 * tq, :] = jnp.where(lane >= B_QK_DIM, q, jnp.zeros_like(q))
    acc_ref[...] = jnp.zeros_like(acc_ref)
    m_ref[...] = jnp.full_like(m_ref, MASK_VALUE)
    l_ref[...] = jnp.zeros_like(l_ref)

    def lane_fold(x, op):
        out = x[:, 0:128]
        for t in range(1, x.shape[1] // 128):
            out = op(out, x[:, t * 128:(t + 1) * 128])
        return out

    def scores(row0, width):
        rows = pl.ds(pl.multiple_of(row0, tq), width)
        return lax.dot_general(qx_ref[...], k_ref[rows, :], (((1,), (1,)), ((), ())), preferred_element_type=F32)

    def absorb(s, row0, width, masked):
        rows = pl.ds(pl.multiple_of(row0, tq), width)
        if masked:
            q_row = i * tq + (_iota2((2 * tq, width), 0) & (tq - 1))
            k_row = row0 + _iota2((2 * tq, width), 1)
            visible = (_row_chunk_id(k_row) <= _row_chunk_id(q_row)) & (k_row >= ROW0 - N_META)
            s = jnp.where(visible, s, MASK_VALUE)
        m_prev = m_ref[...]
        m_new = jnp.maximum(m_prev, jnp.max(lane_fold(s, jnp.maximum), axis=1, keepdims=True))
        alpha = jnp.exp2(m_prev - m_new)
        p = jnp.exp2(s - pltpu.repeat(m_new, width // 128, axis=1))
        l_ref[...] = alpha * l_ref[...] + lane_fold(p, jnp.add)
        m_ref[...] = m_new
        acc_ref[...] = acc_ref[...] * pltpu.repeat(alpha, B_V_DIM // 128, axis=1) + jnp.dot(
            p.astype(BF16), v_ref[rows, :], preferred_element_type=F32)

    def step(row0, width, masked):
        absorb(scores(row0, width), row0, width, masked)

    step(0, tq, True)
    n_full = jnp.maximum(i - 1, 0)

    def pair_start(t):
        return jnp.minimum(tq + 2 * tq * t, lp - 2 * tq)

    def full_pair(t, s_cur):
        s_next = scores(pair_start(t + 1), 2 * tq)
        absorb(s_cur, tq + 2 * tq * t, 2 * tq, False)
        return s_next

    def two_pairs(u, s_cur):
        return full_pair(2 * u + 1, full_pair(2 * u, s_cur))

    n_pairs = n_full // 2
    s_tail = lax.fori_loop(0, n_pairs // 2, two_pairs, scores(pair_start(0), 2 * tq))

    @pl.when(n_pairs % 2 == 1)
    def _():
        absorb(s_tail, tq + 2 * tq * (n_pairs - 1), 2 * tq, False)

    @pl.when((i > 0) & (n_full % 2 == 1))
    def _():
        step((i - 1) * tq, 2 * tq, True)

    @pl.when((i > 0) & (n_full % 2 == 0))
    def _():
        step(i * tq, tq, True)

    lam4 = lam_ref[...]
    lam = (jnp.exp(jnp.sum(lam4[0:1] * lam4[1:2], axis=1, keepdims=True))
           - jnp.exp(jnp.sum(lam4[2:3] * lam4[3:4], axis=1, keepdims=True)) + lambda_init)
    acc = acc_ref[...] / jnp.sum(l_ref[...], axis=1, keepdims=True)
    o = acc[0:tq] - lam * acc[tq:2 * tq]
    o = _rms(o, sg_ref[...], SUBLN_EPS) * (1.0 - lambda_init)
    o_ref[...] = (o * _silu(g_ref[...])).astype(o_ref.dtype)


def _attn_core(qkv, p, lam4, subln_g, d, lambda_init):
    lp = qkv.shape[0]
    heads = d // B_V_DIM
    tq = ROW_TILE
    assert lp >= 3 * tq
    kernel = functools.partial(_attn_kernel, tq=tq, lp=lp, lambda_init=lambda_init)
    return pl.pallas_call(
        kernel, grid=(heads, lp // tq),
        in_specs=[pl.BlockSpec((tq, B_V_DIM), lambda h, i: (i, h)),
                  pl.BlockSpec((lp, B_V_DIM), lambda h, i: (0, heads + h)),
                  pl.BlockSpec((lp, B_V_DIM), lambda h, i: (0, 2 * heads + h)),
                  pl.BlockSpec((tq, B_V_DIM), lambda h, i: (i, 3 * heads + h)),
                  pl.BlockSpec((4, B_QK_DIM), lambda h, i: (0, 0)),
                  pl.BlockSpec((1, B_V_DIM), lambda h, i: (0, 0))],
        out_specs=pl.BlockSpec((tq, B_V_DIM), lambda h, i: (i, h)),
        out_shape=jax.ShapeDtypeStruct((lp, d), BF16),
        scratch_shapes=[pltpu.VMEM((2 * tq, B_V_DIM), BF16), pltpu.VMEM((2 * tq, B_V_DIM), F32),
                        pltpu.VMEM((2 * tq, 128), F32), pltpu.VMEM((2 * tq, 128), F32)],
        compiler_params=_cparams(2), name="diff_attn",
    )(qkv, qkv, qkv, p, lam4, subln_g.reshape(1, B_V_DIM))


def _rope_tables(lp):
    pos = (jnp.arange(lp) - (ROW0 - N_META)).astype(F32)
    inv_freq = ROPE_THETA ** (-jnp.arange(0, ROPE_DIM, 2, dtype=F32) / ROPE_DIM)
    ang = pos[:, None] * inv_freq[None, :]
    cos, sin = jnp.cos(ang), jnp.sin(ang)
    pad = B_QK_DIM - ROPE_DIM
    cos_t = jnp.concatenate([cos, cos, jnp.ones((lp, pad), F32)], axis=1)
    sin_t = jnp.concatenate([-sin, sin, jnp.zeros((lp, pad), F32)], axis=1)
    return cos_t, sin_t


def _shift_mix_kernel(u_ref, prev_ref, mu_ref, o_ref):
    u = u_ref[...]
    last = prev_ref[7:8, :]
    last = jnp.where(pl.program_id(0) == 0, jnp.zeros_like(last), last)
    row = _iota2(u.shape, 0)
    prev = jnp.where(row == 0, last, pltpu.roll(u, 1, 0))
    delta = prev - u
    for s in range(o_ref.shape[0]):
        o_ref[s] = (u + delta * mu_ref[s:s + 1, :]).astype(o_ref.dtype)


def _shift_mix(u, mu):
    lp, d = u.shape
    n = mu.shape[0]
    per8 = ROW_TILE // 8
    return pl.pallas_call(
        _shift_mix_kernel, grid=(lp // ROW_TILE,),
        in_specs=[pl.BlockSpec((ROW_TILE, d), lambda i: (i, 0)),
                  pl.BlockSpec((8, d), lambda i: (jnp.maximum(i * per8 - 1, 0), 0)),
                  pl.BlockSpec((n, d), lambda i: (0, 0))],
        out_specs=pl.BlockSpec((n, ROW_TILE, d), lambda i: (0, i, 0)),
        out_shape=jax.ShapeDtypeStruct((n, lp, d), BF16), compiler_params=_cparams(1), name="shift_mix",
    )(u, u, mu)


RWKV_GROUP = 256 // C_HEAD_DIM
RWKV_GROUP_W = RWKV_GROUP * C_HEAD_DIM


def _rwkv_kernel(r_ref, k_ref, v_ref, g_ref, dw_ref, da_ref, w0_ref, a0_ref, kk_ref, ka_ref, rk_ref,
                 lw_ref, lb_ref, o_ref, st_ref, *, groups, chunks):
    @pl.when(pl.program_id(1) == 0)
    def _():
        st_ref[...] = jnp.zeros_like(st_ref)

    n = C_HEAD_DIM
    gw = RWKV_GROUP_W
    width = groups * gw
    shift = int(math.log2(n))
    ri = _iota2((gw, gw), 0)
    ci = _iota2((gw, gw), 1)
    same = (ri >> shift) == (ci >> shift)
    seg = same.astype(BF16)

    def seg_sum(x):
        return jnp.concatenate([jnp.dot(x[:, g * gw:(g + 1) * gw].astype(BF16), seg, preferred_element_type=F32)
                                for g in range(groups)], axis=1)

    rt = ri & (n - 1)
    ct = ci & (n - 1)
    strict = same & (ct < rt)
    lower = same & (ct <= rt)
    eye = ri == ci

    def tile_rows(y):
        return jnp.concatenate([y] * RWKV_GROUP, axis=0)

    def expand(y):
        return jnp.where(same, tile_rows(y), 0.0)

    def compact(e):
        out = e[0:CHUNK]
        for h in range(1, RWKV_GROUP):
            out = out + e[h * CHUNK:(h + 1) * CHUNK]
        return out

    chains = [(c, g) for c in range(chunks) for g in range(groups)]
    tok = {}
    for c in range(chunks):
        rows = slice(c * CHUNK, (c + 1) * CHUNK)
        r = r_ref[rows, :]
        k = k_ref[rows, :]
        v = v_ref[rows, :]
        logw = -C_DECAY_SCALE * jax.nn.sigmoid(w0_ref[...] + dw_ref[rows, :])
        alpha = jax.nn.sigmoid(a0_ref[...] + da_ref[rows, :])
        kk = k * kk_ref[...]
        kk = kk / jnp.maximum(jnp.sqrt(seg_sum(kk * kk)), 1e-12)
        k2 = k * (1.0 + (alpha - 1.0) * ka_ref[...])
        cum = _chunk_cumsum(logw)
        cum_last = cum[CHUNK - 1:CHUNK, :]
        b = kk * alpha
        e_inv = jnp.exp(-cum)
        e_tail = jnp.exp(cum_last - cum)
        tok[c] = dict(r=r, v=v, k2=k2, p_t=-kk * jnp.exp(cum - logw), r_t=r * jnp.exp(cum), b_t=b * e_inv,
                      k_t=k2 * e_inv, b_h=b * e_tail, k_h=k2 * e_tail, gam=jnp.exp(cum_last))

    def part(c, g, name):
        return tok[c][name][:, g * gw:(g + 1) * gw]

    ep = {cg: expand(part(*cg, "p_t")) for cg in chains}
    ev = {cg: expand(part(*cg, "v")) for cg in chains}
    a_all = {cg: _dot_nt(jnp.concatenate([ep[cg], expand(part(*cg, "r_t"))], axis=0),
                         jnp.concatenate([tile_rows(part(*cg, "b_t")), tile_rows(part(*cg, "k_t"))], axis=0))
             for cg in chains}
    power = {cg: jnp.where(strict, a_all[cg][:gw, :gw], 0.0) for cg in chains}
    l2 = {cg: jnp.where(strict, a_all[cg][:gw, gw:], 0.0) for cg in chains}
    a3 = {cg: jnp.where(lower, a_all[cg][gw:, :gw], 0.0) for cg in chains}
    a4 = {cg: jnp.where(lower, a_all[cg][gw:, gw:], 0.0) for cg in chains}
    t_inv = {cg: jnp.where(eye, 1.0, power[cg]) for cg in chains}
    l2v = {cg: _dot(l2[cg], ev[cg]) for cg in chains}
    a4v = {cg: _dot(a4[cg], ev[cg]) for cg in chains}
    khv = {cg: _dot_tn(part(*cg, "k_h"), part(*cg, "v")) for cg in chains}
    for _ in range(int(math.log2(CHUNK)) - 1):
        power = {cg: _dot(power[cg], power[cg]) for cg in chains}
        t_inv = {cg: t_inv[cg] + _dot(t_inv[cg], power[cg]) for cg in chains}
    wv = {cg: _dot(t_inv[cg], jnp.concatenate([ep[cg], l2v[cg]], axis=1)) for cg in chains}
    qo = {cg: _dot(a3[cg], wv[cg]) for cg in chains}
    mz = {cg: _dot_tn(part(*cg, "b_h"), jnp.concatenate([compact(wv[cg][:, :gw]), compact(wv[cg][:, gw:])], axis=1))
          for cg in chains}

    for c in range(chunks):
        rows = slice(c * CHUNK, (c + 1) * CHUNK)
        outs = []
        for g in range(groups):
            cg = (c, g)
            q_t = part(c, g, "r_t") + compact(qo[cg][:, :gw])
            o_intra = compact(qo[cg][:, gw:] + a4v[cg])
            m_mat = jnp.where(same, mz[cg][:, :gw], 0.0) + jnp.where(eye, part(c, g, "gam"), 0.0)
            z_mat = jnp.where(same, mz[cg][:, gw:] + khv[cg], 0.0)
            st = st_ref[g]
            outs.append(_dot(q_t, st) + o_intra)
            st_ref[g] = _dot(m_mat, st) + z_mat
        o = jnp.concatenate(outs, axis=1)
        mean = seg_sum(o) * (1.0 / n)
        cen = o - mean
        var = seg_sum(cen * cen) * (1.0 / n)
        o = cen * lax.rsqrt(var + LNX_EPS) * lw_ref[...] + lb_ref[...]
        o = o + seg_sum(tok[c]["r"] * tok[c]["k2"] * rk_ref[...]) * tok[c]["v"]
        o_ref[rows, :] = (o * _silu(g_ref[rows, :])).astype(o_ref.dtype)


def _rwkv_core(r, k, v, gate, dw, da, vecs, d):
    lp = r.shape[0]
    groups = 2
    chunks = 4
    width = groups * RWKV_GROUP_W
    rows = chunks * CHUNK
    act = pl.BlockSpec((rows, width), lambda j, c: (c, j))
    vec = pl.BlockSpec((1, width), lambda j, c: (0, j))
    return pl.pallas_call(
        functools.partial(_rwkv_kernel, groups=groups, chunks=chunks), grid=(d // width, lp // rows),
        in_specs=[act] * 6 + [vec] * 7, out_specs=act,
        out_shape=jax.ShapeDtypeStruct((lp, d), BF16),
        scratch_shapes=[pltpu.VMEM((groups, RWKV_GROUP_W, RWKV_GROUP_W), F32)],
        compiler_params=_cparams(2), name="rwkv7",
    )(r, k, v, gate, dw, da, *[t.reshape(1, d) for t in vecs])


def _hgrn2_layer(u, w_in, lb, gnorm_g, w_out, d):
    p = _matmul(u, w_in.astype(BF16), 4 * d)
    return _matmul(_hgrn2_core(p, lb, gnorm_g, d), w_out.astype(BF16), d)


def _attn_layer(u, w_in, lam4, subln_g, w_out, d, lambda_init, tables):
    p = _matmul(u, w_in.astype(BF16), 4 * d)
    qkv = _rope_qkv(p, *tables, d)
    return _matmul(_attn_core(qkv, p, lam4, subln_g, d, lambda_init), w_out.astype(BF16), d)


def _rwkv_layer(u, mu, w_in, w0, w1, w2, a0, a1, a2, k_k, k_a, r_k, lnx_w, lnx_b, w_out, d):
    xs = _shift_mix(u, mu)
    w_in = w_in.astype(BF16)
    r, k, v, gate = (_matmul(xs, w_in, d, col_off=s * d, lead=s) for s in range(4))
    dw = _lora(xs, 4, w1.astype(BF16), w2.astype(BF16), True)
    da = _lora(xs, 5, a1.astype(BF16), a2.astype(BF16), False)
    o = _rwkv_core(r, k, v, gate, dw, da, (w0, a0, k_k, k_a, r_k.reshape(-1), lnx_w, lnx_b), d)
    return _matmul(o, w_out.astype(BF16), d)


def kernel(x, meta_tokens, pre_norm_g, post_norm_g, a_w_in, a_lb_logits, a_gnorm_g, a_w_out, b_w_in, b_lam_q1, b_lam_k1, b_lam_q2, b_lam_k2, b_subln_g, b_w_out, c_mu, c_w_in, c_w0, c_w1, c_w2, c_a0, c_a1, c_a2, c_k_k, c_k_a, c_r_k, c_lnx_w, c_lnx_b, c_w_out):
    bsz, seq, d = x.shape
    depth = pre_norm_g.shape[0]
    assert bsz == 1 and seq % ROW_TILE == 0 and meta_tokens.shape[0] == N_META
    lp = ROW0 + seq
    h = jnp.concatenate([jnp.zeros((ROW0 - N_META, d), F32), meta_tokens.astype(F32), x[0]], axis=0)
    tables = _rope_tables(lp)
    lb_p = jax.nn.softmax(a_lb_logits.astype(F32), axis=0)
    lb_all = jnp.cumsum(lb_p, axis=0) - lb_p[0]

    def u_dtype(i):
        return F32 if i % N_MIXERS == 2 else BF16

    u = _prenorm(h, pre_norm_g[0], u_dtype(0))
    for i in range(depth):
        kind, j = i % N_MIXERS, i // N_MIXERS
        if kind == 0:
            y = _hgrn2_layer(u, a_w_in[j], lb_all[j], a_gnorm_g[j], a_w_out[j], d)
        elif kind == 1:
            lambda_init = 0.8 - 0.6 * math.exp(-0.3 * i)
            lam4 = jnp.stack([b_lam_q1[j], b_lam_k1[j], b_lam_q2[j], b_lam_k2[j]]).astype(F32)
            y = _attn_layer(u, b_w_in[j], lam4, b_subln_g[j], b_w_out[j], d, lambda_init, tables)
        else:
            y = _rwkv_layer(u, c_mu[j], c_w_in[j], c_w0[j], c_w1[j], c_w2[j], c_a0[j], c_a1[j], c_a2[j],
                            c_k_k[j], c_k_a[j], c_r_k[j], c_lnx_w[j], c_lnx_b[j], c_w_out[j], d)
        if i + 1 < depth:
            h, u = _residual_prenorm(h, y, post_norm_g[i], pre_norm_g[i + 1], u_dtype(i + 1))
        else:
            out = _residual_out(h, y, post_norm_g[i], seq)
    return out[None]
```

```python
import functools
import math

import jax
import jax.numpy as jnp
import numpy as np
from jax import lax
from jax.experimental import pallas as pl
from jax.experimental.pallas import tpu as pltpu

F32 = jnp.float32
BF16 = jnp.bfloat16

CHUNK = 64
N_META = 16
N_MIXERS = 3
NORM_EPS = 1e-6
A_HEAD_DIM = 128
B_QK_DIM = 128
B_V_DIM = 2 * B_QK_DIM
ROPE_THETA = 500000.0
ROPE_DIM = B_QK_DIM // 4
SUBLN_EPS = 1e-5
C_HEAD_DIM = 64
C_DECAY_SCALE = 0.606531
LNX_EPS = 64e-5

ROW0 = 256
ROW_TILE = 256
MASK_VALUE = -1e30
NO_CHUNK = 1 << 30
PAIR_UNROLL = 4
Q_SCALE_LOG2 =B_QK_DIM ** -0.5 * math.log2(math.e)
V7X_VMEM_LIMIT = 56 * 1024 * 1024


def _cparams(n_axes, vmem=V7X_VMEM_LIMIT):
    return pltpu.CompilerParams(dimension_semantics=("arbitrary",) * n_axes, vmem_limit_bytes=vmem)


def _dot(a, b):
    return jnp.dot(a.astype(BF16), b.astype(BF16), preferred_element_type=F32)


def _dot_nt(a, b):
    return lax.dot_general(a.astype(BF16), b.astype(BF16), (((1,), (1,)), ((), ())), preferred_element_type=F32)


def _dot_tn(a, b):
    return lax.dot_general(a.astype(BF16), b.astype(BF16), (((0,), (0,)), ((), ())), preferred_element_type=F32)


def _iota2(shape, axis):
    return lax.broadcasted_iota(jnp.int32, shape, axis)


def _chunk_cumsum(x):
    tri = (_iota2((CHUNK, CHUNK), 1) <= _iota2((CHUNK, CHUNK), 0)).astype(BF16)
    hi = x.astype(BF16)
    lo = (x - hi.astype(F32)).astype(BF16)
    return jnp.dot(tri, hi, preferred_element_type=F32) + jnp.dot(tri, lo, preferred_element_type=F32)


def _silu(x):
    return x * jax.nn.sigmoid(x)


def _rms(x, g, eps):
    return x * lax.rsqrt(jnp.mean(x * x, axis=-1, keepdims=True) + eps) * g


def _prenorm_kernel(h_ref, g_ref, u_ref):
    u_ref[...] = _rms(h_ref[...], g_ref[...], NORM_EPS).astype(u_ref.dtype)


def _residual_prenorm_kernel(h_ref, y_ref, gp_ref, g_ref, hn_ref, u_ref):
    hn = h_ref[...] + _rms(y_ref[...], gp_ref[...], NORM_EPS)
    hn_ref[...] = hn
    u_ref[...] = _rms(hn, g_ref[...], NORM_EPS).astype(u_ref.dtype)


def _residual_kernel(h_ref, y_ref, gp_ref, o_ref):
    o_ref[...] = h_ref[...] + _rms(y_ref[...], gp_ref[...], NORM_EPS)


def _prenorm(h, g, u_dtype):
    lp, d = h.shape
    row = pl.BlockSpec((ROW_TILE, d), lambda i: (i, 0))
    vec = pl.BlockSpec((1, d), lambda i: (0, 0))
    return pl.pallas_call(
        _prenorm_kernel, grid=(lp // ROW_TILE,), in_specs=[row, vec], out_specs=row,
        out_shape=jax.ShapeDtypeStruct((lp, d), u_dtype), compiler_params=_cparams(1), name="prenorm",
    )(h, g.reshape(1, d))


def _residual_prenorm(h, y, gp, g, u_dtype):
    lp, d = h.shape
    row = pl.BlockSpec((ROW_TILE, d), lambda i: (i, 0))
    vec = pl.BlockSpec((1, d), lambda i: (0, 0))
    return pl.pallas_call(
        _residual_prenorm_kernel, grid=(lp // ROW_TILE,), in_specs=[row, row, vec, vec], out_specs=[row, row],
        out_shape=[jax.ShapeDtypeStruct((lp, d), F32), jax.ShapeDtypeStruct((lp, d), u_dtype)],
        compiler_params=_cparams(1), name="residual_prenorm",
    )(h, y, gp.reshape(1, d), g.reshape(1, d))


def _residual_out(h, y, gp, seq):
    lp, d = h.shape
    off = ROW0 // ROW_TILE
    row_in = pl.BlockSpec((ROW_TILE, d), lambda i: (i + off, 0))
    vec = pl.BlockSpec((1, d), lambda i: (0, 0))
    return pl.pallas_call(
        _residual_kernel, grid=(seq // ROW_TILE,), in_specs=[row_in, row_in, vec],
        out_specs=pl.BlockSpec((ROW_TILE, d), lambda i: (i, 0)),
        out_shape=jax.ShapeDtypeStruct((seq, d), F32), compiler_params=_cparams(1), name="residual_out",
    )(h, y, gp.reshape(1, d))


def _mm_kernel(a_ref, w_ref, o_ref):
    o_ref[...] = jnp.dot(a_ref[...], w_ref[...], preferred_element_type=F32).astype(o_ref.dtype)


def _row_block(m, cap=1280):
    n = m // ROW_TILE
    best = max(d for d in range(1, n + 1) if n % d == 0 and ROW_TILE * d <= cap)
    return ROW_TILE * best


def _lhs_spec(a, tm, lead, index_map):
    k = a.shape[-1]
    if a.ndim == 2:
        return pl.BlockSpec((tm, k), lambda *g: (index_map(*g), 0))
    return pl.BlockSpec((None, tm, k), lambda *g: (lead, index_map(*g), 0))


def _matmul(a, w, n_out, col_off=0, lead=0, out_dtype=F32, tn=512):
    m = a.shape[-2]
    k = a.shape[-1]
    tm = _row_block(m)
    tn = min(tn, n_out)
    joff = col_off // tn
    return pl.pallas_call(
        _mm_kernel, grid=(m // tm, n_out // tn),
        in_specs=[_lhs_spec(a, tm, lead, lambda i, j: i), pl.BlockSpec((k, tn), lambda i, j: (0, j + joff))],
        out_specs=pl.BlockSpec((tm, tn), lambda i, j: (i, j)),
        out_shape=jax.ShapeDtypeStruct((m, n_out), out_dtype), compiler_params=_cparams(2), name="matmul",
    )(a, w)


def _lora_kernel(x_ref, a_ref, b_ref, o_ref, *, use_tanh):
    mid = jnp.dot(x_ref[...], a_ref[...], preferred_element_type=F32)
    if use_tanh:
        mid = jnp.tanh(mid)
    o_ref[...] = _dot(mid, b_ref[...])


def _lora(x, lead, a, b, use_tanh):
    m, d = x.shape[-2:]
    r = a.shape[1]
    return pl.pallas_call(
        functools.partial(_lora_kernel, use_tanh=use_tanh), grid=(m // ROW_TILE,),
        in_specs=[_lhs_spec(x, ROW_TILE, lead, lambda i: i), pl.BlockSpec((d, r), lambda i: (0, 0)),
                  pl.BlockSpec((r, d), lambda i: (0, 0))],
        out_specs=pl.BlockSpec((ROW_TILE, d), lambda i: (i, 0)),
        out_shape=jax.ShapeDtypeStruct((m, d), F32), compiler_params=_cparams(1), name="lora",
    )(x, a, b)


HGRN2_LEVELS = tuple(CHUNK >> (i + 1) for i in range(int(math.log2(CHUNK))))


def _hgrn2_range_sums():
    t = np.arange(CHUNK)[:, None]
    c = np.arange(CHUNK)[None, :]
    blocks = [c <= t]
    for h in HGRN2_LEVELS:
        r = (t // (2 * h)) * 2 * h + h
        blocks.append(np.where(t >= r, (c > r) & (c <= t), (c > t) & (c <= r)))
    return jnp.asarray(np.concatenate(blocks, axis=0), dtype=BF16)


def _hgrn2_kernel(q_ref, f_ref, i_ref, g_ref, lb_ref, gn_ref, rs_ref, o_ref, st_ref, *, heads, chunks):
    @pl.when(pl.program_id(1) == 0)
    def _():
        st_ref[...] = jnp.zeros_like(st_ref)

    dh = A_HEAD_DIM
    width = heads * dh
    n_lev = len(HGRN2_LEVELS)
    ti = _iota2((CHUNK, CHUNK), 0)
    si = _iota2((CHUNK, CHUNK), 1)
    eye = ti == si
    pair_masks = []
    for h in HGRN2_LEVELS:
        sh = int(math.log2(h))
        pair_masks.append((((ti >> sh) & 1) == 1) & ((ti >> (sh + 1)) == (si >> (sh + 1))) & (((si >> sh) & 1) == 0))
    trow = _iota2((CHUNK, width), 0)
    upper = [((trow >> int(math.log2(h))) & 1) == 1 for h in HGRN2_LEVELS]
    lb = lb_ref[...]
    head_cols = [slice(h * dh, (h + 1) * dh) for h in range(heads)]

    tok = []
    for c in range(chunks):
        rows = slice(c * CHUNK, (c + 1) * CHUNK)
        q = q_ref[rows, :]
        v = i_ref[rows, :]
        f = lb + (1.0 - lb) * jax.nn.sigmoid(f_ref[rows, :])
        g = jnp.log(f)
        k = 1.0 - f
        g_hi = g.astype(BF16)
        g_lo = (g - g_hi.astype(F32)).astype(BF16)
        e2 = jnp.dot(rs_ref[...], jnp.concatenate([g_hi, g_lo], axis=1), preferred_element_type=F32)
        e = e2[:, :width] + e2[:, width:]
        b = e[0:CHUNK]
        b_last = b[CHUNK - 1:CHUNK, :]
        ex = jnp.exp(e)
        xs = [jnp.where(upper[i], q, k) * ex[(i + 1) * CHUNK:(i + 2) * CHUNK] for i in range(n_lev)]
        tok.append(dict(q=q, k=k, v=v, qe=q * ex[0:CHUNK], kd=k * jnp.exp(b_last - b), decay=jnp.exp(b_last), xs=xs))

    pairs = [(c, cs) for c in range(chunks) for cs in head_cols]
    scores = [jnp.where(eye, _dot_nt(tok[c]["q"][:, cs], tok[c]["k"][:, cs]), 0.0) for c, cs in pairs]
    for i in range(n_lev):
        scores = [sc + jnp.where(pair_masks[i], _dot_nt(tok[c]["xs"][i][:, cs], tok[c]["xs"][i][:, cs]), 0.0)
                  for sc, (c, cs) in zip(scores, pairs)]
    intra = [_dot(sc, tok[c]["v"][:, cs]) for sc, (c, cs) in zip(scores, pairs)]
    kv = [_dot_tn(tok[c]["v"][:, cs], tok[c]["kd"][:, cs]) for c, cs in pairs]
    for n, (c, cs) in enumerate(pairs):
        h = n % heads
        rows = slice(c * CHUNK, (c + 1) * CHUNK)
        st = st_ref[h]
        o = _dot_nt(tok[c]["qe"][:, cs], st) + intra[n]
        st_ref[h] = st * tok[c]["decay"][:, cs] + kv[n]
        o = _rms(o, gn_ref[...], NORM_EPS)
        o_ref[rows, cs] = (o * _silu(g_ref[rows, cs])).astype(o_ref.dtype)


def _hgrn2_core(p, lb, gnorm_g, d):
    lp = p.shape[0]
    heads = 4
    width = heads * A_HEAD_DIM
    ncol = d // width
    blk = lambda part: pl.BlockSpec((ROW_TILE, width), lambda j, r: (r, part * ncol + j))
    range_sums = _hgrn2_range_sums()
    return pl.pallas_call(
        functools.partial(_hgrn2_kernel, heads=heads, chunks=ROW_TILE // CHUNK),
        grid=(ncol, lp // ROW_TILE),
        in_specs=[blk(0), blk(1), blk(2), blk(3), pl.BlockSpec((1, width), lambda j, r: (0, j)),
                  pl.BlockSpec((1, A_HEAD_DIM), lambda j, r: (0, 0)),
                  pl.BlockSpec(range_sums.shape, lambda j, r: (0, 0))],
        out_specs=pl.BlockSpec((ROW_TILE, width), lambda j, r: (r, j)),
        out_shape=jax.ShapeDtypeStruct((lp, d), BF16),
        scratch_shapes=[pltpu.VMEM((heads, A_HEAD_DIM, A_HEAD_DIM), F32)],
        compiler_params=_cparams(2), name="hgrn2",
    )(p, p, p, p, lb.reshape(1, d), gnorm_g.reshape(1, A_HEAD_DIM), range_sums)


def _rope_kernel(p_ref, cos_ref, sin_ref, o_ref, *, d):
    part = pl.program_id(1)

    @pl.when(part < 2)
    def _():
        lane = _iota2((ROW_TILE, B_QK_DIM), 1)
        first = lane < ROPE_DIM // 2
        fac = jnp.where(part == 0, Q_SCALE_LOG2, 1.0)
        cos = cos_ref[...] * fac
        sin = sin_ref[...] * fac
        for c in range(d // B_QK_DIM):
            cols = slice(c * B_QK_DIM, (c + 1) * B_QK_DIM)
            x = p_ref[:, cols]
            other = jnp.where(first, pltpu.roll(x, B_QK_DIM - ROPE_DIM // 2, 1), pltpu.roll(x, ROPE_DIM // 2, 1))
            o_ref[:, cols] = (x * cos + other * sin).astype(o_ref.dtype)

    @pl.when(part == 2)
    def _():
        o_ref[...] = p_ref[...].astype(o_ref.dtype)


def _rope_qkv(p, cos_t, sin_t, d):
    lp = p.shape[0]
    return pl.pallas_call(
        functools.partial(_rope_kernel, d=d), grid=(lp // ROW_TILE, 3),
        in_specs=[pl.BlockSpec((ROW_TILE, d), lambda r, c: (r, c)),
                  pl.BlockSpec((ROW_TILE, B_QK_DIM), lambda r, c: (r, 0)),
                  pl.BlockSpec((ROW_TILE, B_QK_DIM), lambda r, c: (r, 0))],
        out_specs=pl.BlockSpec((ROW_TILE, d), lambda r, c: (r, c)),
        out_shape=jax.ShapeDtypeStruct((lp, 3 * d), BF16), compiler_params=_cparams(2), name="rope_qkv",
    )(p, cos_t, sin_t)


def _row_chunk_id(row):
    return jnp.maximum((row - (ROW0 - CHUNK)) >> 6, 0)


def _attn_kernel(q_ref, k_ref, v_ref, g_ref, lam_ref, sg_ref, o_ref, qx_ref, acc_ref, m_ref, l_ref, *,
                 tq, lp, lambda_init):
    i = pl.program_id(1)
    q = q_ref[...]
    lane = _iota2(q.shape, 1)
    qx_ref[0:tq, :] = jnp.where(lane < B_QK_DIM, q, jnp.zeros_like(q))
    qx_ref[tq:2 * tq, :] = jnp.where(lane >= B_QK_DIM, q, jnp.zeros_like(q))
    acc_ref[...] = jnp.zeros_like(acc_ref)
    m_ref[...] = jnp.full_like(m_ref, MASK_VALUE)
    l_ref[...] = jnp.zeros_like(l_ref)

    def lane_fold(x, op):
        out = x[:, 0:128]
        for t in range(1, x.shape[1] // 128):
            out = op(out, x[:, t * 128:(t + 1) * 128])
        return out

    def lane_tile(x, n):
        return jnp.concatenate([x] * n, axis=1)

    def key_rows(row0, width):
        return pl.ds(row0 if isinstance(row0, int) else pl.multiple_of(row0, tq), width)

    def scores(row0, width):
        rows = key_rows(row0, width)
        return lax.dot_general(qx_ref[...], k_ref[rows, :], (((1,), (1,)), ((), ())), preferred_element_type=F32)

    q_chunk = _row_chunk_id(i * tq + (_iota2((2 * tq, 1), 0) & (tq - 1)))

    def masked_scores(row0, width):
        k_row = row0 + _iota2((1, width), 1)
        k_chunk = jnp.where(k_row >= ROW0 - N_META, _row_chunk_id(k_row), NO_CHUNK)
        return jnp.where(k_chunk <= q_chunk, scores(row0, width), MASK_VALUE)

    def soften(s):
        m_prev = m_ref[...]
        m_new = jnp.maximum(m_prev, jnp.max(lane_fold(s, jnp.maximum), axis=1, keepdims=True))
        alpha = jnp.exp2(m_prev - m_new)
        p = jnp.exp2(s - lane_tile(m_new, s.shape[1] // 128))
        l_ref[...] = alpha * l_ref[...] + lane_fold(p, jnp.add)
        m_ref[...] = m_new
        return p.astype(BF16), alpha

    def value_product(p, row0, width):
        return jnp.dot(p, v_ref[key_rows(row0, width), :], preferred_element_type=F32)

    def rescale_add(alpha, pv):
        acc_ref[...] = acc_ref[...] * lane_tile(alpha, B_V_DIM // 128) + pv

    first_key = (ROW0 - N_META) // 128 * 128
    head_width = tq - first_key

    def closing_step(row0, width):
        parts = [masked_scores(first_key, head_width)]
        if width:
            parts.append(masked_scores(row0, width))
        p, alpha = soften(jnp.concatenate(parts, axis=1))
        pv = value_product(p[:, :head_width], first_key, head_width)
        if width:
            pv = pv + value_product(p[:, head_width:], row0, width)
        rescale_add(alpha, pv)

    n_full = jnp.maximum(i - 1, 0)

    def pair_start(t):
        return jnp.minimum(tq + 2 * tq * t, lp - 2 * tq)

    def full_pair(t, s_cur):
        s_next = scores(pair_start(t + 1), 2 * tq)
        p, alpha = soften(s_cur)
        rescale_add(alpha, value_product(p, tq + 2 * tq * t, 2 * tq))
        return s_next

    def four_pairs(u, s_cur):
        for t in range(PAIR_UNROLL):
            s_cur = full_pair(PAIR_UNROLL * u + t, s_cur)
        return s_cur

    n_pairs = n_full // 2
    n_main = n_pairs // PAIR_UNROLL
    s_tail = lax.fori_loop(0, n_main, four_pairs, scores(pair_start(0), 2 * tq))
    lax.fori_loop(n_main * PAIR_UNROLL, n_pairs, full_pair, s_tail)

    @pl.when(i == 0)
    def _():
        closing_step(0, 0)

    @pl.when((i > 0) & (n_full % 2 == 1))
    def _():
        closing_step((i - 1) * tq, 2 * tq)

    @pl.when((i > 0) & (n_full % 2 == 0))
    def _():
        closing_step(i * tq, tq)

    lam4 = lam_ref[...]
    lam = (jnp.exp(jnp.sum(lam4[0:1] * lam4[1:2], axis=1, keepdims=True))
           - jnp.exp(jnp.sum(lam4[2:3] * lam4[3:4], axis=1, keepdims=True)) + lambda_init)
    acc = acc_ref[...] / jnp.sum(l_ref[...], axis=1, keepdims=True)
    o = acc[0:tq] - lam * acc[tq:2 * tq]
    o = _rms(o, sg_ref[...], SUBLN_EPS) * (1.0 - lambda_init)
    o_ref[...] = (o * _silu(g_ref[...])).astype(o_ref.dtype)


def _attn_core(qkv, p, lam4, subln_g, d, lambda_init):
    lp = qkv.shape[0]
    heads = d // B_V_DIM
    tq = ROW_TILE
    assert lp >= 3 * tq
    kernel = functools.partial(_attn_kernel, tq=tq, lp=lp, lambda_init=lambda_init)
    return pl.pallas_call(
        kernel, grid=(heads, lp // tq),
        in_specs=[pl.BlockSpec((tq, B_V_DIM), lambda h, i: (i, h)),
                  pl.BlockSpec((lp, B_V_DIM), lambda h, i: (0, heads + h)),
                  pl.BlockSpec((lp, B_V_DIM), lambda h, i: (0, 2 * heads + h)),
                  pl.BlockSpec((tq, B_V_DIM), lambda h, i: (i, 3 * heads + h)),
                  pl.BlockSpec((4, B_QK_DIM), lambda h, i: (0, 0)),
                  pl.BlockSpec((1, B_V_DIM), lambda h, i: (0, 0))],
        out_specs=pl.BlockSpec((tq, B_V_DIM), lambda h, i: (i, h)),
        out_shape=jax.ShapeDtypeStruct((lp, d), BF16),
        scratch_shapes=[pltpu.VMEM((2 * tq, B_V_DIM), BF16), pltpu.VMEM((2 * tq, B_V_DIM), F32),
                        pltpu.VMEM((2 * tq, 128), F32), pltpu.VMEM((2 * tq, 128), F32)],
        compiler_params=_cparams(2), name="diff_attn",
    )(qkv, qkv, qkv, p, lam4, subln_g.reshape(1, B_V_DIM))


def _rope_tables(lp):
    pos = (jnp.arange(lp) - (ROW0 - N_META)).astype(F32)
    inv_freq = ROPE_THETA ** (-jnp.arange(0, ROPE_DIM, 2, dtype=F32) / ROPE_DIM)
    ang = pos[:, None] * inv_freq[None, :]
    cos, sin = jnp.cos(ang), jnp.sin(ang)
    pad = B_QK_DIM - ROPE_DIM
    cos_t = jnp.concatenate([cos, cos, jnp.ones((lp, pad), F32)], axis=1)
    sin_t = jnp.concatenate([-sin, sin, jnp.zeros((lp, pad), F32)], axis=1)
    return cos_t, sin_t


def _shift_mix_kernel(u_ref, prev_ref, mu_ref, o_ref):
    u = u_ref[...]
    last = prev_ref[7:8, :]
    last = jnp.where(pl.program_id(0) == 0, jnp.zeros_like(last), last)
    row = _iota2(u.shape, 0)
    prev = jnp.where(row == 0, last, pltpu.roll(u, 1, 0))
    delta = prev - u
    for s in range(o_ref.shape[0]):
        o_ref[s] = (u + delta * mu_ref[s:s + 1, :]).astype(o_ref.dtype)


def _shift_mix(u, mu):
    lp, d = u.shape
    n = mu.shape[0]
    per8 = ROW_TILE // 8
    return pl.pallas_call(
        _shift_mix_kernel, grid=(lp // ROW_TILE,),
        in_specs=[pl.BlockSpec((ROW_TILE, d), lambda i: (i, 0)),
                  pl.BlockSpec((8, d), lambda i: (jnp.maximum(i * per8 - 1, 0), 0)),
                  pl.BlockSpec((n, d), lambda i: (0, 0))],
        out_specs=pl.BlockSpec((n, ROW_TILE, d), lambda i: (0, i, 0)),
        out_shape=jax.ShapeDtypeStruct((n, lp, d), BF16), compiler_params=_cparams(1), name="shift_mix",
    )(u, u, mu)


RWKV_GROUP = 256 // C_HEAD_DIM
RWKV_GROUP_W = RWKV_GROUP * C_HEAD_DIM


def _rwkv_kernel(r_ref, k_ref, v_ref, g_ref, dw_ref, da_ref, w0_ref, a0_ref, kk_ref, ka_ref, rk_ref,
                 lw_ref, lb_ref, o_ref, st_ref, *, groups, chunks):
    @pl.when(pl.program_id(1) == 0)
    def _():
        st_ref[...] = jnp.zeros_like(st_ref)

    n = C_HEAD_DIM
    gw = RWKV_GROUP_W
    width = groups * gw
    shift = int(math.log2(n))
    ri = _iota2((gw, gw), 0)
    ci = _iota2((gw, gw), 1)
    same = (ri >> shift) == (ci >> shift)
    seg = same.astype(BF16)

    def seg_sum(x):
        return jnp.concatenate([jnp.dot(x[:, g * gw:(g + 1) * gw].astype(BF16), seg, preferred_element_type=F32)
                                for g in range(groups)], axis=1)

    rt = ri & (n - 1)
    ct = ci & (n - 1)
    strict = same & (ct < rt)
    lower = same & (ct <= rt)
    eye = ri == ci

    def tile_rows(y):
        return jnp.concatenate([y] * RWKV_GROUP, axis=0)

    def expand(y):
        return jnp.where(same, tile_rows(y), 0.0)

    def compact(e):
        out = e[0:CHUNK]
        for h in range(1, RWKV_GROUP):
            out = out + e[h * CHUNK:(h + 1) * CHUNK]
        return out

    chains = [(c, g) for c in range(chunks) for g in range(groups)]
    tok = {}
    for c in range(chunks):
        rows = slice(c * CHUNK, (c + 1) * CHUNK)
        r = r_ref[rows, :]
        k = k_ref[rows, :]
        v = v_ref[rows, :]
        logw = -C_DECAY_SCALE * jax.nn.sigmoid(w0_ref[...] + dw_ref[rows, :])
        alpha = jax.nn.sigmoid(a0_ref[...] + da_ref[rows, :])
        kk = k * kk_ref[...]
        kk = kk / jnp.maximum(jnp.sqrt(seg_sum(kk * kk)), 1e-12)
        k2 = k * (1.0 + (alpha - 1.0) * ka_ref[...])
        cum = _chunk_cumsum(logw)
        cum_last = cum[CHUNK - 1:CHUNK, :]
        b = kk * alpha
        e_inv = jnp.exp(-cum)
        e_tail = jnp.exp(cum_last - cum)
        tok[c] = dict(r=r, v=v, k2=k2, p_t=-kk * jnp.exp(cum - logw), r_t=r * jnp.exp(cum), b_t=b * e_inv,
                      k_t=k2 * e_inv, b_h=b * e_tail, k_h=k2 * e_tail, gam=jnp.exp(cum_last))

    def part(c, g, name):
        return tok[c][name][:, g * gw:(g + 1) * gw]

    ep = {cg: expand(part(*cg, "p_t")) for cg in chains}
    ev = {cg: expand(part(*cg, "v")) for cg in chains}
    a_all = {cg: _dot_nt(jnp.concatenate([ep[cg], expand(part(*cg, "r_t"))], axis=0),
                         jnp.concatenate([tile_rows(part(*cg, "b_t")), tile_rows(part(*cg, "k_t"))], axis=0))
             for cg in chains}
    power = {cg: jnp.where(strict, a_all[cg][:gw, :gw], 0.0) for cg in chains}
    l2 = {cg: jnp.where(strict, a_all[cg][:gw, gw:], 0.0) for cg in chains}
    a3 = {cg: jnp.where(lower, a_all[cg][gw:, :gw], 0.0) for cg in chains}
    a4 = {cg: jnp.where(lower, a_all[cg][gw:, gw:], 0.0) for cg in chains}
    t_inv = {cg: jnp.where(eye, 1.0, power[cg]) for cg in chains}
    l2v = {cg: _dot(l2[cg], ev[cg]) for cg in chains}
    a4v = {cg: _dot(a4[cg], ev[cg]) for cg in chains}
    khv = {cg: _dot_tn(part(*cg, "k_h"), part(*cg, "v")) for cg in chains}
    for _ in range(int(math.log2(CHUNK)) - 1):
        power = {cg: _dot(power[cg], power[cg]) for cg in chains}
        t_inv = {cg: t_inv[cg] + _dot(t_inv[cg], power[cg]) for cg in chains}
    wv = {cg: _dot(t_inv[cg], jnp.concatenate([ep[cg], l2v[cg]], axis=1)) for cg in chains}
    qo = {cg: _dot(a3[cg], wv[cg]) for cg in chains}
    mz = {cg: _dot_tn(part(*cg, "b_h"), jnp.concatenate([compact(wv[cg][:, :gw]), compact(wv[cg][:, gw:])], axis=1))
          for cg in chains}

    for c in range(chunks):
        rows = slice(c * CHUNK, (c + 1) * CHUNK)
        outs = []
        for g in range(groups):
            cg = (c, g)
            q_t = part(c, g, "r_t") + compact(qo[cg][:, :gw])
            o_intra = compact(qo[cg][:, gw:] + a4v[cg])
            m_mat = jnp.where(same, mz[cg][:, :gw], 0.0) + jnp.where(eye, part(c, g, "gam"), 0.0)
            z_mat = jnp.where(same, mz[cg][:, gw:] + khv[cg], 0.0)
            st = st_ref[g]
            outs.append(_dot(q_t, st) + o_intra)
            st_ref[g] = _dot(m_mat, st) + z_mat
        o = jnp.concatenate(outs, axis=1)
        mean = seg_sum(o) * (1.0 / n)
        cen = o - mean
        var = seg_sum(cen * cen) * (1.0 / n)
        o = cen * lax.rsqrt(var + LNX_EPS) * lw_ref[...] + lb_ref[...]
        o = o + seg_sum(tok[c]["r"] * tok[c]["k2"] * rk_ref[...]) * tok[c]["v"]
        o_ref[rows, :] = (o * _silu(g_ref[rows, :])).astype(o_ref.dtype)


def _rwkv_core(r, k, v, gate, dw, da, vecs, d):
    lp = r.shape[0]
    groups = 2
    chunks = 4
    width = groups * RWKV_GROUP_W
    rows = chunks * CHUNK
    act = pl.BlockSpec((rows, width), lambda j, c: (c, j))
    vec = pl.BlockSpec((1, width), lambda j, c: (0, j))
    return pl.pallas_call(
        functools.partial(_rwkv_kernel, groups=groups, chunks=chunks), grid=(d // width, lp // rows),
        in_specs=[act] * 6 + [vec] * 7, out_specs=act,
        out_shape=jax.ShapeDtypeStruct((lp, d), BF16),
        scratch_shapes=[pltpu.VMEM((groups, RWKV_GROUP_W, RWKV_GROUP_W), F32)],
        compiler_params=_cparams(2), name="rwkv7",
    )(r, k, v, gate, dw, da, *[t.reshape(1, d) for t in vecs])


def _hgrn2_layer(u, w_in, lb, gnorm_g, w_out, d):
    p = _matmul(u, w_in.astype(BF16), 4 * d)
    return _matmul(_hgrn2_core(p, lb, gnorm_g, d), w_out.astype(BF16), d)


def _attn_layer(u, w_in, lam4, subln_g, w_out, d, lambda_init, tables):
    p = _matmul(u, w_in.astype(BF16), 4 * d)
    qkv = _rope_qkv(p, *tables, d)
    return _matmul(_attn_core(qkv, p, lam4, subln_g, d, lambda_init), w_out.astype(BF16), d)


def _rwkv_layer(u, mu, w_in, w0, w1, w2, a0, a1, a2, k_k, k_a, r_k, lnx_w, lnx_b, w_out, d):
    xs = _shift_mix(u, mu)
    w_in = w_in.astype(BF16)
    r, k, v, gate = (_matmul(xs, w_in, d, col_off=s * d, lead=s) for s in range(4))
    dw = _lora(xs, 4, w1.astype(BF16), w2.astype(BF16), True)
    da = _lora(xs, 5, a1.astype(BF16), a2.astype(BF16), False)
    o = _rwkv_core(r, k, v, gate, dw, da, (w0, a0, k_k, k_a, r_k.reshape(-1), lnx_w, lnx_b), d)
    return _matmul(o, w_out.astype(BF16), d)


def kernel(x, meta_tokens, pre_norm_g, post_norm_g, a_w_in, a_lb_logits, a_gnorm_g, a_w_out, b_w_in, b_lam_q1, b_lam_k1, b_lam_q2, b_lam_k2, b_subln_g, b_w_out, c_mu, c_w_in, c_w0, c_w1, c_w2, c_a0, c_a1, c_a2, c_k_k, c_k_a, c_r_k, c_lnx_w, c_lnx_b, c_w_out):
    bsz, seq, d = x.shape
    depth = pre_norm_g.shape[0]
    assert bsz == 1 and seq % ROW_TILE == 0 and meta_tokens.shape[0] == N_META
    lp = ROW0 + seq
    h = jnp.concatenate([jnp.zeros((ROW0 - N_META, d), F32), meta_tokens.astype(F32), x[0]], axis=0)
    tables = _rope_tables(lp)
    lb_p = jax.nn.softmax(a_lb_logits.astype(F32), axis=0)
    lb_all = jnp.cumsum(lb_p, axis=0) - lb_p[0]

    def u_dtype(i):
        return F32 if i % N_MIXERS == 2 else BF16

    u = _prenorm(h, pre_norm_g[0], u_dtype(0))
    for i in range(depth):
        kind, j = i % N_MIXERS, i // N_MIXERS
        if kind == 0:
            y = _hgrn2_layer(u, a_w_in[j], lb_all[j], a_gnorm_g[j], a_w_out[j], d)
        elif kind == 1:
            lambda_init = 0.8 - 0.6 * math.exp(-0.3 * i)
            lam4 = jnp.stack([b_lam_q1[j], b_lam_k1[j], b_lam_q2[j], b_lam_k2[j]]).astype(F32)
            y = _attn_layer(u, b_w_in[j], lam4, b_subln_g[j], b_w_out[j], d, lambda_init, tables)
        else:
            y = _rwkv_layer(u, c_mu[j], c_w_in[j], c_w0[j], c_w1[j], c_w2[j], c_a0[j], c_a1[j], c_a2[j],
                            c_k_k[j], c_k_a[j], c_r_k[j], c_lnx_w[j], c_lnx_b[j], c_w_out[j], d)
        if i + 1 < depth:
            h, u = _residual_prenorm(h, y, post_norm_g[i], pre_norm_g[i + 1], u_dtype(i + 1))
        else:
            out = _residual_out(h, y, post_norm_g[i], seq)
    return out[None]
```

```python
import functools
import math

import jax
import jax.numpy as jnp
import numpy as np
from jax import lax
from jax.experimental import pallas as pl
from jax.experimental.pallas import tpu as pltpu

F32 = jnp.float32
BF16 = jnp.bfloat16

CHUNK = 64
N_META = 16
N_MIXERS = 3
NORM_EPS = 1e-6
A_HEAD_DIM = 128
B_QK_DIM = 128
B_V_DIM = 2 * B_QK_DIM
ROPE_THETA = 500000.0
ROPE_DIM = B_QK_DIM // 4
SUBLN_EPS = 1e-5
C_HEAD_DIM = 64
C_DECAY_SCALE = 0.606531
LNX_EPS = 64e-5

ROW0 = 512
ROW_TILE = 256
ATTN_TILE = 512
MASK_VALUE = -1e30
NO_CHUNK = 1 << 30
TILE_UNROLL = 4
Q_SCALE_LOG2 =B_QK_DIM ** -0.5 * math.log2(math.e)
V7X_VMEM_LIMIT = 56 * 1024 * 1024


def _cparams(n_axes, vmem=V7X_VMEM_LIMIT):
    return pltpu.CompilerParams(dimension_semantics=("arbitrary",) * n_axes, vmem_limit_bytes=vmem)


def _dot(a, b):
    return jnp.dot(a.astype(BF16), b.astype(BF16), preferred_element_type=F32)


def _dot_nt(a, b):
    return lax.dot_general(a.astype(BF16), b.astype(BF16), (((1,), (1,)), ((), ())), preferred_element_type=F32)


def _dot_tn(a, b):
    return lax.dot_general(a.astype(BF16), b.astype(BF16), (((0,), (0,)), ((), ())), preferred_element_type=F32)


def _iota2(shape, axis):
    return lax.broadcasted_iota(jnp.int32, shape, axis)


def _chunk_cumsum(x):
    tri = (_iota2((CHUNK, CHUNK), 1) <= _iota2((CHUNK, CHUNK), 0)).astype(BF16)
    hi = x.astype(BF16)
    lo = (x - hi.astype(F32)).astype(BF16)
    return jnp.dot(tri, hi, preferred_element_type=F32) + jnp.dot(tri, lo, preferred_element_type=F32)


def _silu(x):
    return x * jax.nn.sigmoid(x)


def _rms(x, g, eps):
    return x * lax.rsqrt(jnp.mean(x * x, axis=-1, keepdims=True) + eps) * g


HEAD_TILES = ROW0 // ROW_TILE


def _stream_rows(refs):
    if len(refs) == 1:
        return refs[0][...]
    head_ref, frames_ref = refs
    return jnp.where(pl.program_id(0) < HEAD_TILES, head_ref[...], frames_ref[...])


def _stream_specs(h):
    d = h[0].shape[1]
    if len(h) == 1:
        return [pl.BlockSpec((ROW_TILE, d), lambda i: (i, 0))]
    return [pl.BlockSpec((ROW_TILE, d), lambda i: (jnp.minimum(i, HEAD_TILES - 1), 0)),
            pl.BlockSpec((ROW_TILE, d), lambda i: (jnp.maximum(i - HEAD_TILES, 0), 0))]


def _prenorm_kernel(*refs, n_h):
    g_ref, u_ref = refs[n_h:]
    u_ref[...] = _rms(_stream_rows(refs[:n_h]), g_ref[...], NORM_EPS).astype(u_ref.dtype)


def _residual_prenorm_kernel(*refs, n_h):
    y_ref, gp_ref, g_ref, hn_ref, u_ref = refs[n_h:]
    hn = _stream_rows(refs[:n_h]) + _rms(y_ref[...], gp_ref[...], NORM_EPS)
    hn_ref[...] = hn
    u_ref[...] = _rms(hn, g_ref[...], NORM_EPS).astype(u_ref.dtype)


def _residual_shiftmix_kernel(h_ref, y_ref, hp_ref, yp_ref, gp_ref, g_ref, mu_ref, hn_ref, xs_ref):
    hn = h_ref[...] + _rms(y_ref[...], gp_ref[...], NORM_EPS)
    hn_ref[...] = hn
    u = _rms(hn, g_ref[...], NORM_EPS)
    last = _rms(hp_ref[7:8, :] + _rms(yp_ref[7:8, :], gp_ref[...], NORM_EPS), g_ref[...], NORM_EPS)
    last = jnp.where(pl.program_id(0) == 0, jnp.zeros_like(last), last)
    prev = jnp.where(_iota2(u.shape, 0) == 0, last, pltpu.roll(u, 1, 0))
    delta = prev - u
    for s in range(xs_ref.shape[0]):
        xs_ref[s] = (u + delta * mu_ref[s:s + 1, :]).astype(xs_ref.dtype)


def _residual_kernel(h_ref, y_ref, gp_ref, o_ref):
    o_ref[...] = h_ref[...] + _rms(y_ref[...], gp_ref[...], NORM_EPS)


def _prenorm(h, g, u_dtype):
    d = g.shape[0]
    lp = h[0].shape[0] if len(h) == 1 else ROW0 + h[1].shape[0]
    row = pl.BlockSpec((ROW_TILE, d), lambda i: (i, 0))
    vec = pl.BlockSpec((1, d), lambda i: (0, 0))
    return pl.pallas_call(
        functools.partial(_prenorm_kernel, n_h=len(h)), grid=(lp // ROW_TILE,),
        in_specs=_stream_specs(h) + [vec], out_specs=row,
        out_shape=jax.ShapeDtypeStruct((lp, d), u_dtype), compiler_params=_cparams(1), name="prenorm",
    )(*h, g.reshape(1, d))


def _residual_prenorm(h, y, gp, g, u_dtype):
    lp, d = y.shape
    row = pl.BlockSpec((ROW_TILE, d), lambda i: (i, 0))
    vec = pl.BlockSpec((1, d), lambda i: (0, 0))
    return pl.pallas_call(
        functools.partial(_residual_prenorm_kernel, n_h=len(h)), grid=(lp // ROW_TILE,),
        in_specs=_stream_specs(h) + [row, vec, vec], out_specs=[row, row],
        out_shape=[jax.ShapeDtypeStruct((lp, d), F32), jax.ShapeDtypeStruct((lp, d), u_dtype)],
        compiler_params=_cparams(1), name="residual_prenorm",
    )(*h, y, gp.reshape(1, d), g.reshape(1, d))


def _residual_shiftmix(h, y, gp, g, mu):
    lp, d = y.shape
    n = mu.shape[0]
    tile = ROW_TILE // 2
    per8 = tile // 8
    row = pl.BlockSpec((tile, d), lambda i: (i, 0))
    before = pl.BlockSpec((8, d), lambda i: (jnp.maximum(i * per8 - 1, 0), 0))
    vec = pl.BlockSpec((1, d), lambda i: (0, 0))
    return pl.pallas_call(
        _residual_shiftmix_kernel, grid=(lp // tile,),
        in_specs=[row, row, before, before, vec, vec, pl.BlockSpec((n, d), lambda i: (0, 0))],
        out_specs=[row, pl.BlockSpec((n, tile, d), lambda i: (0, i, 0))],
        out_shape=[jax.ShapeDtypeStruct((lp, d), F32), jax.ShapeDtypeStruct((n, lp, d), BF16)],
        compiler_params=_cparams(1), name="residual_shiftmix",
    )(h, y, h, y, gp.reshape(1, d), g.reshape(1, d), mu)


def _residual_out(h, y, gp, seq):
    lp, d = h.shape
    off = ROW0 // ROW_TILE
    row_in = pl.BlockSpec((ROW_TILE, d), lambda i: (i + off, 0))
    vec = pl.BlockSpec((1, d), lambda i: (0, 0))
    return pl.pallas_call(
        _residual_kernel, grid=(seq // ROW_TILE,), in_specs=[row_in, row_in, vec],
        out_specs=pl.BlockSpec((ROW_TILE, d), lambda i: (i, 0)),
        out_shape=jax.ShapeDtypeStruct((seq, d), F32), compiler_params=_cparams(1), name="residual_out",
    )(h, y, gp.reshape(1, d))


def _mm_kernel(a_ref, w_ref, o_ref):
    o_ref[...] = jnp.dot(a_ref[...], w_ref[...], preferred_element_type=F32).astype(o_ref.dtype)


def _row_block(m, cap=1536):
    n = m // ROW_TILE
    best = max(d for d in range(1, n + 1) if n % d == 0 and ROW_TILE * d <= cap)
    return ROW_TILE * best


def _lhs_spec(a, tm, lead, index_map):
    k = a.shape[-1]
    if a.ndim == 2:
        return pl.BlockSpec((tm, k), lambda *g: (index_map(*g), 0))
    return pl.BlockSpec((None, tm, k), lambda *g: (lead, index_map(*g), 0))


def _matmul(a, w, n_out, col_off=0, lead=0, out_dtype=F32, tn=512):
    m = a.shape[-2]
    k = a.shape[-1]
    tm = _row_block(m)
    tn = min(tn, n_out)
    joff = col_off // tn
    return pl.pallas_call(
        _mm_kernel, grid=(m // tm, n_out // tn),
        in_specs=[_lhs_spec(a, tm, lead, lambda i, j: i), pl.BlockSpec((k, tn), lambda i, j: (0, j + joff))],
        out_specs=pl.BlockSpec((tm, tn), lambda i, j: (i, j)),
        out_shape=jax.ShapeDtypeStruct((m, n_out), out_dtype), compiler_params=_cparams(2), name="matmul",
    )(a, w)


def _rope(x, cos, sin):
    first = _iota2(x.shape, 1) < ROPE_DIM // 2
    other = jnp.where(first, pltpu.roll(x, B_QK_DIM - ROPE_DIM // 2, 1), pltpu.roll(x, ROPE_DIM // 2, 1))
    return x * cos + other * sin


def _rope_kernel(p_ref, cos_ref, sin_ref, o_ref):
    fac = jnp.where(pl.program_id(1) == 0, Q_SCALE_LOG2, 1.0)
    cos = cos_ref[...] * fac
    sin = sin_ref[...] * fac
    for c in range(p_ref.shape[1] // B_QK_DIM):
        cols = slice(c * B_QK_DIM, (c + 1) * B_QK_DIM)
        o_ref[:, cols] = _rope(p_ref[:, cols].astype(F32), cos, sin).astype(o_ref.dtype)


def _rope_qk(p, cos_t, sin_t, d):
    lp = p.shape[0]
    table = pl.BlockSpec((ROW_TILE, B_QK_DIM), lambda r, c: (r, 0))
    return pl.pallas_call(
        _rope_kernel, grid=(lp // ROW_TILE, 2),
        in_specs=[pl.BlockSpec((ROW_TILE, d), lambda r, c: (r, c)), table, table],
        out_specs=pl.BlockSpec((ROW_TILE, d), lambda r, c: (r, c)),
        out_shape=jax.ShapeDtypeStruct((lp, 2 * d), BF16), compiler_params=_cparams(2), name="rope_qk",
    )(p, cos_t, sin_t)


def _lora_kernel(x_ref, a_ref, b_ref, o_ref, *, use_tanh):
    mid = jnp.dot(x_ref[...], a_ref[...], preferred_element_type=F32)
    if use_tanh:
        mid = jnp.tanh(mid)
    o_ref[...] = _dot(mid, b_ref[...])


def _lora(x, lead, a, b, use_tanh):
    m, d = x.shape[-2:]
    r = a.shape[1]
    return pl.pallas_call(
        functools.partial(_lora_kernel, use_tanh=use_tanh), grid=(m // ROW_TILE,),
        in_specs=[_lhs_spec(x, ROW_TILE, lead, lambda i: i), pl.BlockSpec((d, r), lambda i: (0, 0)),
                  pl.BlockSpec((r, d), lambda i: (0, 0))],
        out_specs=pl.BlockSpec((ROW_TILE, d), lambda i: (i, 0)),
        out_shape=jax.ShapeDtypeStruct((m, d), F32), compiler_params=_cparams(1), name="lora",
    )(x, a, b)


HGRN2_LEVELS = tuple(CHUNK >> (i + 1) for i in range(int(math.log2(CHUNK))))


def _hgrn2_range_sums():
    t = np.arange(CHUNK)[:, None]
    c = np.arange(CHUNK)[None, :]
    blocks = [c <= t]
    for h in HGRN2_LEVELS:
        r = (t // (2 * h)) * 2 * h + h
        blocks.append(np.where(t >= r, (c > r) & (c <= t), (c > t) & (c <= r)))
    return jnp.asarray(np.concatenate(blocks, axis=0), dtype=BF16)


def _hgrn2_kernel(q_ref, f_ref, i_ref, g_ref, lb_ref, gn_ref, rs_ref, o_ref, st_ref, *, heads, chunks):
    @pl.when(pl.program_id(1) == 0)
    def _():
        st_ref[...] = jnp.zeros_like(st_ref)

    dh = A_HEAD_DIM
    width = heads * dh
    n_lev = len(HGRN2_LEVELS)
    ti = _iota2((CHUNK, CHUNK), 0)
    si = _iota2((CHUNK, CHUNK), 1)
    eye = ti == si
    pair_masks = []
    for h in HGRN2_LEVELS:
        sh = int(math.log2(h))
        pair_masks.append((((ti >> sh) & 1) == 1) & ((ti >> (sh + 1)) == (si >> (sh + 1))) & (((si >> sh) & 1) == 0))
    trow = _iota2((CHUNK, width), 0)
    upper = [((trow >> int(math.log2(h))) & 1) == 1 for h in HGRN2_LEVELS]
    lb = lb_ref[...]
    head_cols = [slice(h * dh, (h + 1) * dh) for h in range(heads)]

    tok = []
    for c in range(chunks):
        rows = slice(c * CHUNK, (c + 1) * CHUNK)
        q = q_ref[rows, :]
        v = i_ref[rows, :]
        f = lb + (1.0 - lb) * jax.nn.sigmoid(f_ref[rows, :])
        g = jnp.log(f)
        k = 1.0 - f
        g_hi = g.astype(BF16)
        g_lo = (g - g_hi.astype(F32)).astype(BF16)
        e2 = jnp.dot(rs_ref[...], jnp.concatenate([g_hi, g_lo], axis=1), preferred_element_type=F32)
        e = e2[:, :width] + e2[:, width:]
        b = e[0:CHUNK]
        b_last = b[CHUNK - 1:CHUNK, :]
        ex = jnp.exp(e)
        xs = [jnp.where(upper[i], q, k) * ex[(i + 1) * CHUNK:(i + 2) * CHUNK] for i in range(n_lev)]
        tok.append(dict(q=q, k=k, v=v, qe=q * ex[0:CHUNK], kd=k * jnp.exp(b_last - b), decay=jnp.exp(b_last), xs=xs))

    pairs = [(c, cs) for c in range(chunks) for cs in head_cols]
    scores = [jnp.where(eye, _dot_nt(tok[c]["q"][:, cs], tok[c]["k"][:, cs]), 0.0) for c, cs in pairs]
    for i in range(n_lev):
        scores = [sc + jnp.where(pair_masks[i], _dot_nt(tok[c]["xs"][i][:, cs], tok[c]["xs"][i][:, cs]), 0.0)
                  for sc, (c, cs) in zip(scores, pairs)]
    intra = [_dot(sc, tok[c]["v"][:, cs]) for sc, (c, cs) in zip(scores, pairs)]
    kv = [_dot_tn(tok[c]["v"][:, cs], tok[c]["kd"][:, cs]) for c, cs in pairs]
    for n, (c, cs) in enumerate(pairs):
        h = n % heads
        rows = slice(c * CHUNK, (c + 1) * CHUNK)
        st = st_ref[h]
        o = _dot_nt(tok[c]["qe"][:, cs], st) + intra[n]
        st_ref[h] = st * tok[c]["decay"][:, cs] + kv[n]
        o = _rms(o, gn_ref[...], NORM_EPS)
        o_ref[rows, cs] = (o * _silu(g_ref[rows, cs])).astype(o_ref.dtype)


def _hgrn2_core(p, lb, gnorm_g, d):
    lp = p.shape[0]
    heads = 4
    width = heads * A_HEAD_DIM
    ncol = d // width
    blk = lambda part: pl.BlockSpec((ROW_TILE, width), lambda j, r: (r, part * ncol + j))
    range_sums = _hgrn2_range_sums()
    return pl.pallas_call(
        functools.partial(_hgrn2_kernel, heads=heads, chunks=ROW_TILE // CHUNK),
        grid=(ncol, lp // ROW_TILE),
        in_specs=[blk(0), blk(1), blk(2), blk(3), pl.BlockSpec((1, width), lambda j, r: (0, j)),
                  pl.BlockSpec((1, A_HEAD_DIM), lambda j, r: (0, 0)),
                  pl.BlockSpec(range_sums.shape, lambda j, r: (0, 0))],
        out_specs=pl.BlockSpec((ROW_TILE, width), lambda j, r: (r, j)),
        out_shape=jax.ShapeDtypeStruct((lp, d), BF16),
        scratch_shapes=[pltpu.VMEM((heads, A_HEAD_DIM, A_HEAD_DIM), F32)],
        compiler_params=_cparams(2), name="hgrn2",
    )(p, p, p, p, lb.reshape(1, d), gnorm_g.reshape(1, A_HEAD_DIM), range_sums)


def _row_chunk_id(row):
    return jnp.maximum((row - (ROW0 - CHUNK)) >> 6, 0)


def _attn_kernel(q_ref, k_ref, v_ref, g_ref, lam_ref, sg_ref, o_ref, qx_ref, acc_ref, m_ref, l_ref, *,
                 tq, lambda_init):
    i = pl.program_id(1)
    q = q_ref[...]
    lane = _iota2(q.shape, 1)
    qx_ref[0:tq, :] = jnp.where(lane < B_QK_DIM, q, jnp.zeros_like(q))
    qx_ref[tq:2 * tq, :] = jnp.where(lane >= B_QK_DIM, q, jnp.zeros_like(q))
    acc_ref[...] = jnp.zeros_like(acc_ref)
    m_ref[...] = jnp.full_like(m_ref, MASK_VALUE)
    l_ref[...] = jnp.zeros_like(l_ref)

    def lane_fold(x, op):
        out = x[:, 0:128]
        for t in range(1, x.shape[1] // 128):
            out = op(out, x[:, t * 128:(t + 1) * 128])
        return out

    def lane_tile(x, n):
        return jnp.concatenate([x] * n, axis=1)

    def key_rows(row0, width):
        return pl.ds(row0 if isinstance(row0, int) else pl.multiple_of(row0, tq), width)

    def scores(row0, width):
        rows = key_rows(row0, width)
        return lax.dot_general(qx_ref[...], k_ref[rows, :], (((1,), (1,)), ((), ())), preferred_element_type=F32)

    q_chunk = _row_chunk_id(i * tq + (_iota2((2 * tq, 1), 0) & (tq - 1)))

    def mask_scores(s, row0):
        k_row = row0 + _iota2((1, s.shape[1]), 1)
        k_chunk = jnp.where(k_row >= ROW0 - N_META, _row_chunk_id(k_row), NO_CHUNK)
        return jnp.where(k_chunk <= q_chunk, s, MASK_VALUE)

    def soften(s):
        m_prev = m_ref[...]
        m_new = jnp.maximum(m_prev, jnp.max(lane_fold(s, jnp.maximum), axis=1, keepdims=True))
        alpha = jnp.exp2(m_prev - m_new)
        p = jnp.exp2(s - lane_tile(m_new, s.shape[1] // 128))
        l_ref[...] = alpha * l_ref[...] + lane_fold(p, jnp.add)
        m_ref[...] = m_new
        return p.astype(BF16), alpha

    def value_product(p, row0, width):
        return jnp.dot(p, v_ref[key_rows(row0, width), :], preferred_element_type=F32)

    def rescale_add(alpha, pv):
        acc_ref[...] = acc_ref[...] * lane_tile(alpha, B_V_DIM // 128) + pv

    first_key = (ROW0 - N_META) // 128 * 128
    head_width = tq - first_key

    def closing_step(s_diag):
        parts = [mask_scores(scores(first_key, head_width), first_key)]
        if s_diag is not None:
            parts.append(mask_scores(s_diag, i * tq))
        p, alpha = soften(jnp.concatenate(parts, axis=1))
        pv = value_product(p[:, :head_width], first_key, head_width)
        if s_diag is not None:
            pv = pv + value_product(p[:, head_width:], i * tq, tq)
        rescale_add(alpha, pv)

    n_full = jnp.maximum(i - 1, 0)

    def tile_start(t):
        return tq * (t + 1)

    def full_tile(t, s_cur):
        s_next = scores(tile_start(t + 1), tq)
        p, alpha = soften(s_cur)
        rescale_add(alpha, value_product(p, tq * (t + 1), tq))
        return s_next

    def unrolled_tiles(u, s_cur):
        for t in range(TILE_UNROLL):
            s_cur = full_tile(TILE_UNROLL * u + t, s_cur)
        return s_cur

    n_main = n_full // TILE_UNROLL
    s_tail = lax.fori_loop(0, n_main, unrolled_tiles, scores(tile_start(0), tq))
    s_diag = lax.fori_loop(n_main * TILE_UNROLL, n_full, full_tile, s_tail)

    @pl.when(i == 0)
    def _():
        closing_step(None)

    @pl.when(i > 0)
    def _():
        closing_step(s_diag)

    lam4 = lam_ref[...]
    lam = (jnp.exp(jnp.sum(lam4[0:1] * lam4[1:2], axis=1, keepdims=True))
           - jnp.exp(jnp.sum(lam4[2:3] * lam4[3:4], axis=1, keepdims=True)) + lambda_init)
    acc = acc_ref[...] / jnp.sum(l_ref[...], axis=1, keepdims=True)
    o = acc[0:tq] - lam * acc[tq:2 * tq]
    o = _rms(o, sg_ref[...], SUBLN_EPS) * (1.0 - lambda_init)
    o_ref[...] = (o * _silu(g_ref[...].astype(F32))).astype(o_ref.dtype)


def _attn_core(qk, p, lam4, subln_g, d, lambda_init):
    lp = p.shape[0]
    heads = d // B_V_DIM
    tq = ATTN_TILE
    assert ROW0 == tq and lp % tq == 0 and lp >= 2 * tq
    kernel = functools.partial(_attn_kernel, tq=tq, lambda_init=lambda_init)
    return pl.pallas_call(
        kernel, grid=(heads, lp // tq),
        in_specs=[pl.BlockSpec((tq, B_V_DIM), lambda h, i: (i, h)),
                  pl.BlockSpec((lp, B_V_DIM), lambda h, i: (0, heads + h)),
                  pl.BlockSpec((lp, B_V_DIM), lambda h, i: (0, 2 * heads + h)),
                  pl.BlockSpec((tq, B_V_DIM), lambda h, i: (i, 3 * heads + h)),
                  pl.BlockSpec((4, B_QK_DIM), lambda h, i: (0, 0)),
                  pl.BlockSpec((1, B_V_DIM), lambda h, i: (0, 0))],
        out_specs=pl.BlockSpec((tq, B_V_DIM), lambda h, i: (i, h)),
        out_shape=jax.ShapeDtypeStruct((lp, d), BF16),
        scratch_shapes=[pltpu.VMEM((2 * tq, B_V_DIM), BF16), pltpu.VMEM((2 * tq, B_V_DIM), F32),
                        pltpu.VMEM((2 * tq, 128), F32), pltpu.VMEM((2 * tq, 128), F32)],
        compiler_params=_cparams(2), name="diff_attn",
    )(qk, qk, p, p, lam4, subln_g.reshape(1, B_V_DIM))


def _rope_tables(lp):
    pos = (jnp.arange(lp) - (ROW0 - N_META)).astype(F32)
    inv_freq = ROPE_THETA ** (-jnp.arange(0, ROPE_DIM, 2, dtype=F32) / ROPE_DIM)
    ang = pos[:, None] * inv_freq[None, :]
    cos, sin = jnp.cos(ang), jnp.sin(ang)
    pad = B_QK_DIM - ROPE_DIM
    cos_t = jnp.concatenate([cos, cos, jnp.ones((lp, pad), F32)], axis=1)
    sin_t = jnp.concatenate([-sin, sin, jnp.zeros((lp, pad), F32)], axis=1)
    return cos_t, sin_t


RWKV_GROUP = 256 // C_HEAD_DIM
RWKV_GROUP_W = RWKV_GROUP * C_HEAD_DIM


def _rwkv_kernel(r_ref, k_ref, v_ref, g_ref, dw_ref, da_ref, w0_ref, a0_ref, kk_ref, ka_ref, rk_ref,
                 lw_ref, lb_ref, o_ref, st_ref, *, groups, chunks):
    @pl.when(pl.program_id(1) == 0)
    def _():
        st_ref[...] = jnp.zeros_like(st_ref)

    n = C_HEAD_DIM
    gw = RWKV_GROUP_W
    width = groups * gw
    shift = int(math.log2(n))
    ri = _iota2((gw, gw), 0)
    ci = _iota2((gw, gw), 1)
    same = (ri >> shift) == (ci >> shift)
    seg = same.astype(BF16)

    def seg_sum(x):
        return jnp.concatenate([jnp.dot(x[:, g * gw:(g + 1) * gw].astype(BF16), seg, preferred_element_type=F32)
                                for g in range(groups)], axis=1)

    rt = ri & (n - 1)
    ct = ci & (n - 1)
    strict = same & (ct < rt)
    lower = same & (ct <= rt)
    eye = ri == ci

    def tile_rows(y):
        return jnp.concatenate([y] * RWKV_GROUP, axis=0)

    def expand(y):
        return jnp.where(same, tile_rows(y), 0.0)

    def compact(e):
        out = e[0:CHUNK]
        for h in range(1, RWKV_GROUP):
            out = out + e[h * CHUNK:(h + 1) * CHUNK]
        return out

    chains = [(c, g) for c in range(chunks) for g in range(groups)]
    tok = {}
    for c in range(chunks):
        rows = slice(c * CHUNK, (c + 1) * CHUNK)
        r = r_ref[rows, :]
        k = k_ref[rows, :]
        v = v_ref[rows, :]
        logw = -C_DECAY_SCALE * jax.nn.sigmoid(w0_ref[...] + dw_ref[rows, :])
        alpha = jax.nn.sigmoid(a0_ref[...] + da_ref[rows, :])
        kk = k * kk_ref[...]
        kk = kk / jnp.maximum(jnp.sqrt(seg_sum(kk * kk)), 1e-12)
        k2 = k * (1.0 + (alpha - 1.0) * ka_ref[...])
        cum = _chunk_cumsum(logw)
        cum_last = cum[CHUNK - 1:CHUNK, :]
        b = kk * alpha
        e_inv = jnp.exp(-cum)
        e_tail = jnp.exp(cum_last - cum)
        tok[c] = dict(r=r, v=v, k2=k2, p_t=-kk * jnp.exp(cum - logw), r_t=r * jnp.exp(cum), b_t=b * e_inv,
                      k_t=k2 * e_inv, b_h=b * e_tail, k_h=k2 * e_tail, gam=jnp.exp(cum_last))

    def part(c, g, name):
        return tok[c][name][:, g * gw:(g + 1) * gw]

    ep = {cg: expand(part(*cg, "p_t")) for cg in chains}
    ev = {cg: expand(part(*cg, "v")) for cg in chains}
    a_all = {cg: _dot_nt(jnp.concatenate([ep[cg], expand(part(*cg, "r_t"))], axis=0),
                         jnp.concatenate([tile_rows(part(*cg, "b_t")), tile_rows(part(*cg, "k_t"))], axis=0))
             for cg in chains}
    power = {cg: jnp.where(strict, a_all[cg][:gw, :gw], 0.0) for cg in chains}
    l2 = {cg: jnp.where(strict, a_all[cg][:gw, gw:], 0.0) for cg in chains}
    a3 = {cg: jnp.where(lower, a_all[cg][gw:, :gw], 0.0) for cg in chains}
    a4 = {cg: jnp.where(lower, a_all[cg][gw:, gw:], 0.0) for cg in chains}
    t_inv = {cg: jnp.where(eye, 1.0, power[cg]) for cg in chains}
    l2v = {cg: _dot(l2[cg], ev[cg]) for cg in chains}
    a4v = {cg: _dot(a4[cg], ev[cg]) for cg in chains}
    khv = {cg: _dot_tn(part(*cg, "k_h"), part(*cg, "v")) for cg in chains}
    for _ in range(int(math.log2(CHUNK)) - 1):
        power = {cg: _dot(power[cg], power[cg]) for cg in chains}
        t_inv = {cg: t_inv[cg] + _dot(t_inv[cg], power[cg]) for cg in chains}
    wv = {cg: _dot(t_inv[cg], jnp.concatenate([ep[cg], l2v[cg]], axis=1)) for cg in chains}
    qo = {cg: _dot(a3[cg], wv[cg]) for cg in chains}
    mz = {cg: _dot_tn(part(*cg, "b_h"), jnp.concatenate([compact(wv[cg][:, :gw]), compact(wv[cg][:, gw:])], axis=1))
          for cg in chains}

    for c in range(chunks):
        rows = slice(c * CHUNK, (c + 1) * CHUNK)
        outs = []
        for g in range(groups):
            cg = (c, g)
            q_t = part(c, g, "r_t") + compact(qo[cg][:, :gw])
            o_intra = compact(qo[cg][:, gw:] + a4v[cg])
            m_mat = jnp.where(same, mz[cg][:, :gw], 0.0) + jnp.where(eye, part(c, g, "gam"), 0.0)
            z_mat = jnp.where(same, mz[cg][:, gw:] + khv[cg], 0.0)
            st = st_ref[g]
            outs.append(_dot(q_t, st) + o_intra)
            st_ref[g] = _dot(m_mat, st) + z_mat
        o = jnp.concatenate(outs, axis=1)
        mean = seg_sum(o) * (1.0 / n)
        cen = o - mean
        var = seg_sum(cen * cen) * (1.0 / n)
        o = cen * lax.rsqrt(var + LNX_EPS) * lw_ref[...] + lb_ref[...]
        o = o + seg_sum(tok[c]["r"] * tok[c]["k2"] * rk_ref[...]) * tok[c]["v"]
        o_ref[rows, :] = (o * _silu(g_ref[rows, :])).astype(o_ref.dtype)


def _rwkv_core(r, k, v, gate, dw, da, vecs, d):
    lp = r.shape[0]
    groups = 2
    chunks = 4
    width = groups * RWKV_GROUP_W
    rows = chunks * CHUNK
    act = pl.BlockSpec((rows, width), lambda j, c: (c, j))
    vec = pl.BlockSpec((1, width), lambda j, c: (0, j))
    return pl.pallas_call(
        functools.partial(_rwkv_kernel, groups=groups, chunks=chunks), grid=(d // width, lp // rows),
        in_specs=[act] * 6 + [vec] * 7, out_specs=act,
        out_shape=jax.ShapeDtypeStruct((lp, d), BF16),
        scratch_shapes=[pltpu.VMEM((groups, RWKV_GROUP_W, RWKV_GROUP_W), F32)],
        compiler_params=_cparams(2), name="rwkv7",
    )(r, k, v, gate, dw, da, *[t.reshape(1, d) for t in vecs])


def _hgrn2_layer(u, w_in, lb, gnorm_g, w_out, d):
    p = _matmul(u, w_in.astype(BF16), 4 * d)
    return _matmul(_hgrn2_core(p, lb, gnorm_g, d), w_out.astype(BF16), d)


def _attn_layer(u, w_in, lam4, subln_g, w_out, d, lambda_init, tables):
    p = _matmul(u, w_in.astype(BF16), 4 * d, out_dtype=BF16)
    qk = _rope_qk(p, *tables, d)
    return _matmul(_attn_core(qk, p, lam4, subln_g, d, lambda_init), w_out.astype(BF16), d)


def _rwkv_layer(xs, w_in, w0, w1, w2, a0, a1, a2, k_k, k_a, r_k, lnx_w, lnx_b, w_out, d):
    w_in = w_in.astype(BF16)
    r, k, v, gate = (_matmul(xs, w_in, d, col_off=s * d, lead=s) for s in range(4))
    dw = _lora(xs, 4, w1.astype(BF16), w2.astype(BF16), True)
    da = _lora(xs, 5, a1.astype(BF16), a2.astype(BF16), False)
    o = _rwkv_core(r, k, v, gate, dw, da, (w0, a0, k_k, k_a, r_k.reshape(-1), lnx_w, lnx_b), d)
    return _matmul(o, w_out.astype(BF16), d)


def kernel(x, meta_tokens, pre_norm_g, post_norm_g, a_w_in, a_lb_logits, a_gnorm_g, a_w_out, b_w_in, b_lam_q1, b_lam_k1, b_lam_q2, b_lam_k2, b_subln_g, b_w_out, c_mu, c_w_in, c_w0, c_w1, c_w2, c_a0, c_a1, c_a2, c_k_k, c_k_a, c_r_k, c_lnx_w, c_lnx_b, c_w_out):
    bsz, seq, d = x.shape
    depth = pre_norm_g.shape[0]
    assert bsz == 1 and seq % ATTN_TILE == 0 and meta_tokens.shape[0] == N_META and depth >= 2
    lp = ROW0 + seq
    h = (jnp.concatenate([jnp.zeros((ROW0 - N_META, d), F32), meta_tokens.astype(F32)], axis=0), x[0])
    tables = _rope_tables(lp)
    lb_p = jax.nn.softmax(a_lb_logits.astype(F32), axis=0)
    lb_all = jnp.cumsum(lb_p, axis=0) - lb_p[0]

    u = _prenorm(h, pre_norm_g[0], BF16)
    for i in range(depth):
        kind, j = i % N_MIXERS, i // N_MIXERS
        if kind == 0:
            y = _hgrn2_layer(u, a_w_in[j], lb_all[j], a_gnorm_g[j], a_w_out[j], d)
        elif kind == 1:
            lambda_init = 0.8 - 0.6 * math.exp(-0.3 * i)
            lam4 = jnp.stack([b_lam_q1[j], b_lam_k1[j], b_lam_q2[j], b_lam_k2[j]]).astype(F32)
            y = _attn_layer(u, b_w_in[j], lam4, b_subln_g[j], b_w_out[j], d, lambda_init, tables)
        else:
            y = _rwkv_layer(u, c_w_in[j], c_w0[j], c_w1[j], c_w2[j], c_a0[j], c_a1[j], c_a2[j],
                            c_k_k[j], c_k_a[j], c_r_k[j], c_lnx_w[j], c_lnx_b[j], c_w_out[j], d)
        if i + 1 == depth:
            out = _residual_out(h[0], y, post_norm_g[i], seq)
        elif (i + 1) % N_MIXERS == 2:
            hn, u = _residual_shiftmix(h[0], y, post_norm_g[i], pre_norm_g[i + 1], c_mu[(i + 1) // N_MIXERS])
            h = (hn,)
        else:
            hn, u = _residual_prenorm(h, y, post_norm_g[i], pre_norm_g[i + 1], BF16)
            h = (hn,)
    return out[None]
```

```python
import functools
import math

import jax
import jax.numpy as jnp
import numpy as np
from jax import lax
from jax.experimental import pallas as pl
from jax.experimental.pallas import tpu as pltpu

F32 = jnp.float32
BF16 = jnp.bfloat16

CHUNK = 64
N_META = 16
N_MIXERS = 3
NORM_EPS = 1e-6
A_HEAD_DIM = 128
B_QK_DIM = 128
B_V_DIM = 2 * B_QK_DIM
ROPE_THETA = 500000.0
ROPE_DIM = B_QK_DIM // 4
SUBLN_EPS = 1e-5
C_HEAD_DIM = 64
C_DECAY_SCALE = 0.606531
LNX_EPS = 64e-5

ROW0 = 512
ROW_TILE = 256
ATTN_TILE = 512
MASK_VALUE = -1e30
NO_CHUNK = 1 << 30
TILE_UNROLL = 4
Q_SCALE_LOG2 =B_QK_DIM ** -0.5 * math.log2(math.e)
V7X_VMEM_LIMIT = 56 * 1024 * 1024


def _cparams(n_axes, vmem=V7X_VMEM_LIMIT):
    return pltpu.CompilerParams(dimension_semantics=("arbitrary",) * n_axes, vmem_limit_bytes=vmem)


def _dot(a, b):
    return jnp.dot(a.astype(BF16), b.astype(BF16), preferred_element_type=F32)


def _dot_nt(a, b):
    return lax.dot_general(a.astype(BF16), b.astype(BF16), (((1,), (1,)), ((), ())), preferred_element_type=F32)


def _dot_tn(a, b):
    return lax.dot_general(a.astype(BF16), b.astype(BF16), (((0,), (0,)), ((), ())), preferred_element_type=F32)


def _iota2(shape, axis):
    return lax.broadcasted_iota(jnp.int32, shape, axis)


def _chunk_cumsum(x):
    tri = (_iota2((CHUNK, CHUNK), 1) <= _iota2((CHUNK, CHUNK), 0)).astype(BF16)
    hi = x.astype(BF16)
    lo = (x - hi.astype(F32)).astype(BF16)
    return jnp.dot(tri, hi, preferred_element_type=F32) + jnp.dot(tri, lo, preferred_element_type=F32)


def _silu(x):
    return x * jax.nn.sigmoid(x)


def _rms(x, g, eps):
    return x * lax.rsqrt(jnp.mean(x * x, axis=-1, keepdims=True) + eps) * g


HEAD_TILES = ROW0 // ROW_TILE


def _stream_rows(refs):
    if len(refs) == 1:
        return refs[0][...]
    head_ref, frames_ref = refs
    return jnp.where(pl.program_id(0) < HEAD_TILES, head_ref[...], frames_ref[...])


def _stream_specs(h):
    d = h[0].shape[1]
    if len(h) == 1:
        return [pl.BlockSpec((ROW_TILE, d), lambda i: (i, 0))]
    return [pl.BlockSpec((ROW_TILE, d), lambda i: (jnp.minimum(i, HEAD_TILES - 1), 0)),
            pl.BlockSpec((ROW_TILE, d), lambda i: (jnp.maximum(i - HEAD_TILES, 0), 0))]


def _prenorm_kernel(*refs, n_h):
    g_ref, u_ref = refs[n_h:]
    u_ref[...] = _rms(_stream_rows(refs[:n_h]), g_ref[...], NORM_EPS).astype(u_ref.dtype)


def _residual_prenorm_kernel(*refs, n_h):
    y_ref, gp_ref, g_ref, hn_ref, u_ref = refs[n_h:]
    hn = _stream_rows(refs[:n_h]) + _rms(y_ref[...], gp_ref[...], NORM_EPS)
    hn_ref[...] = hn
    u_ref[...] = _rms(hn, g_ref[...], NORM_EPS).astype(u_ref.dtype)


def _residual_shiftmix_kernel(h_ref, y_ref, hp_ref, yp_ref, gp_ref, g_ref, mu_ref, hn_ref, xs_ref):
    hn = h_ref[...] + _rms(y_ref[...], gp_ref[...], NORM_EPS)
    hn_ref[...] = hn
    u = _rms(hn, g_ref[...], NORM_EPS)
    last = _rms(hp_ref[7:8, :] + _rms(yp_ref[7:8, :], gp_ref[...], NORM_EPS), g_ref[...], NORM_EPS)
    last = jnp.where(pl.program_id(0) == 0, jnp.zeros_like(last), last)
    prev = jnp.where(_iota2(u.shape, 0) == 0, last, pltpu.roll(u, 1, 0))
    delta = prev - u
    for s in range(xs_ref.shape[0]):
        xs_ref[s] = (u + delta * mu_ref[s:s + 1, :]).astype(xs_ref.dtype)


def _residual_kernel(h_ref, y_ref, gp_ref, o_ref):
    o_ref[...] = h_ref[...] + _rms(y_ref[...], gp_ref[...], NORM_EPS)


def _prenorm(h, g, u_dtype):
    d = g.shape[0]
    lp = h[0].shape[0] if len(h) == 1 else ROW0 + h[1].shape[0]
    row = pl.BlockSpec((ROW_TILE, d), lambda i: (i, 0))
    vec = pl.BlockSpec((1, d), lambda i: (0, 0))
    return pl.pallas_call(
        functools.partial(_prenorm_kernel, n_h=len(h)), grid=(lp // ROW_TILE,),
        in_specs=_stream_specs(h) + [vec], out_specs=row,
        out_shape=jax.ShapeDtypeStruct((lp, d), u_dtype), compiler_params=_cparams(1), name="prenorm",
    )(*h, g.reshape(1, d))


def _residual_prenorm(h, y, gp, g, u_dtype):
    lp, d = y.shape
    row = pl.BlockSpec((ROW_TILE, d), lambda i: (i, 0))
    vec = pl.BlockSpec((1, d), lambda i: (0, 0))
    return pl.pallas_call(
        functools.partial(_residual_prenorm_kernel, n_h=len(h)), grid=(lp // ROW_TILE,),
        in_specs=_stream_specs(h) + [row, vec, vec], out_specs=[row, row],
        out_shape=[jax.ShapeDtypeStruct((lp, d), F32), jax.ShapeDtypeStruct((lp, d), u_dtype)],
        compiler_params=_cparams(1), name="residual_prenorm",
    )(*h, y, gp.reshape(1, d), g.reshape(1, d))


def _residual_shiftmix(h, y, gp, g, mu):
    lp, d = y.shape
    n = mu.shape[0]
    tile = ROW_TILE // 2
    per8 = tile // 8
    row = pl.BlockSpec((tile, d), lambda i: (i, 0))
    before = pl.BlockSpec((8, d), lambda i: (jnp.maximum(i * per8 - 1, 0), 0))
    vec = pl.BlockSpec((1, d), lambda i: (0, 0))
    return pl.pallas_call(
        _residual_shiftmix_kernel, grid=(lp // tile,),
        in_specs=[row, row, before, before, vec, vec, pl.BlockSpec((n, d), lambda i: (0, 0))],
        out_specs=[row, pl.BlockSpec((n, tile, d), lambda i: (0, i, 0))],
        out_shape=[jax.ShapeDtypeStruct((lp, d), F32), jax.ShapeDtypeStruct((n, lp, d), BF16)],
        compiler_params=_cparams(1), name="residual_shiftmix",
    )(h, y, h, y, gp.reshape(1, d), g.reshape(1, d), mu)


def _residual_out(h, y, gp, seq):
    lp, d = h.shape
    off = ROW0 // ROW_TILE
    row_in = pl.BlockSpec((ROW_TILE, d), lambda i: (i + off, 0))
    vec = pl.BlockSpec((1, d), lambda i: (0, 0))
    return pl.pallas_call(
        _residual_kernel, grid=(seq // ROW_TILE,), in_specs=[row_in, row_in, vec],
        out_specs=pl.BlockSpec((ROW_TILE, d), lambda i: (i, 0)),
        out_shape=jax.ShapeDtypeStruct((seq, d), F32), compiler_params=_cparams(1), name="residual_out",
    )(h, y, gp.reshape(1, d))


def _mm_kernel(a_ref, w_ref, o_ref):
    o_ref[...] = jnp.dot(a_ref[...], w_ref[...].astype(BF16), preferred_element_type=F32).astype(o_ref.dtype)


def _row_block(m, cap=1536):
    n = m // ROW_TILE
    best = max(d for d in range(1, n + 1) if n % d == 0 and ROW_TILE * d <= cap)
    return ROW_TILE * best


def _lhs_spec(a, tm, lead, index_map):
    k = a.shape[-1]
    if a.ndim == 2:
        return pl.BlockSpec((tm, k), lambda *g: (index_map(*g), 0))
    return pl.BlockSpec((None, tm, k), lambda *g: (lead, index_map(*g), 0))


def _matmul(a, w, layer, n_out, col_off=0, lead=0, out_dtype=F32, tn=512):
    m = a.shape[-2]
    k = a.shape[-1]
    tm = _row_block(m)
    tn = min(tn, n_out)
    joff = col_off // tn
    return pl.pallas_call(
        _mm_kernel, grid=(m // tm, n_out // tn),
        in_specs=[_lhs_spec(a, tm, lead, lambda i, j: i),
                  pl.BlockSpec((None, k, tn), lambda i, j: (layer, 0, j + joff))],
        out_specs=pl.BlockSpec((tm, tn), lambda i, j: (i, j)),
        out_shape=jax.ShapeDtypeStruct((m, n_out), out_dtype), compiler_params=_cparams(2), name="matmul",
    )(a, w)


def _rope(x, cos, sin):
    first = _iota2(x.shape, 1) < ROPE_DIM // 2
    other = jnp.where(first, pltpu.roll(x, B_QK_DIM - ROPE_DIM // 2, 1), pltpu.roll(x, ROPE_DIM // 2, 1))
    return x * cos + other * sin


def _rope_kernel(p_ref, cos_ref, sin_ref, o_ref):
    fac = jnp.where(pl.program_id(1) == 0, Q_SCALE_LOG2, 1.0)
    cos = cos_ref[...] * fac
    sin = sin_ref[...] * fac
    for c in range(p_ref.shape[1] // B_QK_DIM):
        cols = slice(c * B_QK_DIM, (c + 1) * B_QK_DIM)
        o_ref[:, cols] = _rope(p_ref[:, cols].astype(F32), cos, sin).astype(o_ref.dtype)


def _rope_qk(p, cos_t, sin_t, d):
    lp = p.shape[0]
    table = pl.BlockSpec((ROW_TILE, B_QK_DIM), lambda r, c: (r, 0))
    return pl.pallas_call(
        _rope_kernel, grid=(lp // ROW_TILE, 2),
        in_specs=[pl.BlockSpec((ROW_TILE, d), lambda r, c: (r, c)), table, table],
        out_specs=pl.BlockSpec((ROW_TILE, d), lambda r, c: (r, c)),
        out_shape=jax.ShapeDtypeStruct((lp, 2 * d), BF16), compiler_params=_cparams(2), name="rope_qk",
    )(p, cos_t, sin_t)


def _lora_kernel(x_ref, a_ref, b_ref, o_ref, *, use_tanh):
    mid = jnp.dot(x_ref[...], a_ref[...], preferred_element_type=F32)
    if use_tanh:
        mid = jnp.tanh(mid)
    o_ref[...] = _dot(mid, b_ref[...])


def _lora(x, lead, a, b, use_tanh):
    m, d = x.shape[-2:]
    r = a.shape[1]
    return pl.pallas_call(
        functools.partial(_lora_kernel, use_tanh=use_tanh), grid=(m // ROW_TILE,),
        in_specs=[_lhs_spec(x, ROW_TILE, lead, lambda i: i), pl.BlockSpec((d, r), lambda i: (0, 0)),
                  pl.BlockSpec((r, d), lambda i: (0, 0))],
        out_specs=pl.BlockSpec((ROW_TILE, d), lambda i: (i, 0)),
        out_shape=jax.ShapeDtypeStruct((m, d), F32), compiler_params=_cparams(1), name="lora",
    )(x, a, b)


HGRN2_LEVELS = tuple(CHUNK >> (i + 1) for i in range(int(math.log2(CHUNK))))


def _hgrn2_range_sums():
    t = np.arange(CHUNK)[:, None]
    c = np.arange(CHUNK)[None, :]
    blocks = [c <= t]
    for h in HGRN2_LEVELS:
        r = (t // (2 * h)) * 2 * h + h
        blocks.append(np.where(t >= r, (c > r) & (c <= t), (c > t) & (c <= r)))
    return jnp.asarray(np.concatenate(blocks, axis=0), dtype=BF16)


def _hgrn2_kernel(q_ref, f_ref, i_ref, g_ref, lb_ref, gn_ref, rs_ref, o_ref, st_ref, *, heads, chunks):
    @pl.when(pl.program_id(1) == 0)
    def _():
        st_ref[...] = jnp.zeros_like(st_ref)

    dh = A_HEAD_DIM
    width = heads * dh
    n_lev = len(HGRN2_LEVELS)
    ti = _iota2((CHUNK, CHUNK), 0)
    si = _iota2((CHUNK, CHUNK), 1)
    eye = ti == si
    pair_masks = []
    for h in HGRN2_LEVELS:
        sh = int(math.log2(h))
        pair_masks.append((((ti >> sh) & 1) == 1) & ((ti >> (sh + 1)) == (si >> (sh + 1))) & (((si >> sh) & 1) == 0))
    trow = _iota2((CHUNK, width), 0)
    upper = [((trow >> int(math.log2(h))) & 1) == 1 for h in HGRN2_LEVELS]
    lb = lb_ref[...]
    head_cols = [slice(h * dh, (h + 1) * dh) for h in range(heads)]

    tok = []
    for c in range(chunks):
        rows = slice(c * CHUNK, (c + 1) * CHUNK)
        q = q_ref[rows, :]
        v = i_ref[rows, :]
        f = lb + (1.0 - lb) * jax.nn.sigmoid(f_ref[rows, :])
        g = jnp.log(f)
        k = 1.0 - f
        g_hi = g.astype(BF16)
        g_lo = (g - g_hi.astype(F32)).astype(BF16)
        e2 = jnp.dot(rs_ref[...], jnp.concatenate([g_hi, g_lo], axis=1), preferred_element_type=F32)
        e = e2[:, :width] + e2[:, width:]
        b = e[0:CHUNK]
        b_last = b[CHUNK - 1:CHUNK, :]
        ex = jnp.exp(e)
        xs = [jnp.where(upper[i], q, k) * ex[(i + 1) * CHUNK:(i + 2) * CHUNK] for i in range(n_lev)]
        tok.append(dict(q=q, k=k, v=v, qe=q * ex[0:CHUNK], kd=k * jnp.exp(b_last - b), decay=jnp.exp(b_last), xs=xs))

    pairs = [(c, cs) for c in range(chunks) for cs in head_cols]
    scores = [jnp.where(eye, _dot_nt(tok[c]["q"][:, cs], tok[c]["k"][:, cs]), 0.0) for c, cs in pairs]
    for i in range(n_lev):
        scores = [sc + jnp.where(pair_masks[i], _dot_nt(tok[c]["xs"][i][:, cs], tok[c]["xs"][i][:, cs]), 0.0)
                  for sc, (c, cs) in zip(scores, pairs)]
    intra = [_dot(sc, tok[c]["v"][:, cs]) for sc, (c, cs) in zip(scores, pairs)]
    kv = [_dot_tn(tok[c]["v"][:, cs], tok[c]["kd"][:, cs]) for c, cs in pairs]
    for n, (c, cs) in enumerate(pairs):
        h = n % heads
        rows = slice(c * CHUNK, (c + 1) * CHUNK)
        st = st_ref[h]
        o = _dot_nt(tok[c]["qe"][:, cs], st) + intra[n]
        st_ref[h] = st * tok[c]["decay"][:, cs] + kv[n]
        o = _rms(o, gn_ref[...], NORM_EPS)
        o_ref[rows, cs] = (o * _silu(g_ref[rows, cs])).astype(o_ref.dtype)


def _hgrn2_core(p, lb, gnorm_g, d):
    lp = p.shape[0]
    heads = 4
    width = heads * A_HEAD_DIM
    ncol = d // width
    blk = lambda part: pl.BlockSpec((ROW_TILE, width), lambda j, r: (r, part * ncol + j))
    range_sums = _hgrn2_range_sums()
    return pl.pallas_call(
        functools.partial(_hgrn2_kernel, heads=heads, chunks=ROW_TILE // CHUNK),
        grid=(ncol, lp // ROW_TILE),
        in_specs=[blk(0), blk(1), blk(2), blk(3), pl.BlockSpec((1, width), lambda j, r: (0, j)),
                  pl.BlockSpec((1, A_HEAD_DIM), lambda j, r: (0, 0)),
                  pl.BlockSpec(range_sums.shape, lambda j, r: (0, 0))],
        out_specs=pl.BlockSpec((ROW_TILE, width), lambda j, r: (r, j)),
        out_shape=jax.ShapeDtypeStruct((lp, d), BF16),
        scratch_shapes=[pltpu.VMEM((heads, A_HEAD_DIM, A_HEAD_DIM), F32)],
        compiler_params=_cparams(2), name="hgrn2",
    )(p, p, p, p, lb.reshape(1, d), gnorm_g.reshape(1, A_HEAD_DIM), range_sums)


def _row_chunk_id(row):
    return jnp.maximum((row - (ROW0 - CHUNK)) >> 6, 0)


def _attn_kernel(q_ref, k_ref, v_ref, g_ref, lam_ref, sg_ref, o_ref, qx_ref, acc_ref, m_ref, l_ref, *,
                 tq, lambda_init):
    i = pl.program_id(1)
    q = q_ref[...]
    lane = _iota2(q.shape, 1)
    qx_ref[0:tq, :] = jnp.where(lane < B_QK_DIM, q, jnp.zeros_like(q))
    qx_ref[tq:2 * tq, :] = jnp.where(lane >= B_QK_DIM, q, jnp.zeros_like(q))
    acc_ref[...] = jnp.zeros_like(acc_ref)
    m_ref[...] = jnp.full_like(m_ref, MASK_VALUE)
    l_ref[...] = jnp.zeros_like(l_ref)

    def lane_fold(x, op):
        out = x[:, 0:128]
        for t in range(1, x.shape[1] // 128):
            out = op(out, x[:, t * 128:(t + 1) * 128])
        return out

    def lane_tile(x, n):
        return jnp.concatenate([x] * n, axis=1)

    def key_rows(row0, width):
        return pl.ds(row0 if isinstance(row0, int) else pl.multiple_of(row0, tq), width)

    def scores(row0, width):
        rows = key_rows(row0, width)
        return lax.dot_general(qx_ref[...], k_ref[rows, :], (((1,), (1,)), ((), ())), preferred_element_type=F32)

    q_chunk = _row_chunk_id(i * tq + (_iota2((2 * tq, 1), 0) & (tq - 1)))

    def mask_scores(s, row0):
        k_row = row0 + _iota2((1, s.shape[1]), 1)
        k_chunk = jnp.where(k_row >= ROW0 - N_META, _row_chunk_id(k_row), NO_CHUNK)
        return jnp.where(k_chunk <= q_chunk, s, MASK_VALUE)

    def soften(s):
        m_prev = m_ref[...]
        m_new = jnp.maximum(m_prev, jnp.max(lane_fold(s, jnp.maximum), axis=1, keepdims=True))
        alpha = jnp.exp2(m_prev - m_new)
        p = jnp.exp2(s - lane_tile(m_new, s.shape[1] // 128))
        l_ref[...] = alpha * l_ref[...] + lane_fold(p, jnp.add)
        m_ref[...] = m_new
        return p.astype(BF16), alpha

    def value_product(p, row0, width):
        return jnp.dot(p, v_ref[key_rows(row0, width), :], preferred_element_type=F32)

    def rescale_add(alpha, pv):
        acc_ref[...] = acc_ref[...] * lane_tile(alpha, B_V_DIM // 128) + pv

    first_key = (ROW0 - N_META) // 128 * 128
    head_width = tq - first_key

    def closing_step(s_last):
        row0 = jnp.maximum(i, 1) * tq
        p, alpha = soften(jnp.concatenate(
            [mask_scores(scores(first_key, head_width), first_key), mask_scores(s_last, row0)], axis=1))
        rescale_add(alpha, value_product(p[:, :head_width], first_key, head_width)
                    + value_product(p[:, head_width:], row0, tq))

    n_full = jnp.maximum(i - 1, 0)

    def tile_start(t):
        return tq * (t + 1)

    def full_tile(t, s_cur):
        s_next = scores(tile_start(t + 1), tq)
        p, alpha = soften(s_cur)
        rescale_add(alpha, value_product(p, tq * (t + 1), tq))
        return s_next

    def unrolled_tiles(u, s_cur):
        for t in range(TILE_UNROLL):
            s_cur = full_tile(TILE_UNROLL * u + t, s_cur)
        return s_cur

    n_main = n_full // TILE_UNROLL
    s_tail = lax.fori_loop(0, n_main, unrolled_tiles, scores(tile_start(0), tq))
    s_diag = lax.fori_loop(n_main * TILE_UNROLL, n_full, full_tile, s_tail)

    closing_step(s_diag)

    lam4 = lam_ref[...]
    lam = (jnp.exp(jnp.sum(lam4[0:1] * lam4[1:2], axis=1, keepdims=True))
           - jnp.exp(jnp.sum(lam4[2:3] * lam4[3:4], axis=1, keepdims=True)) + lambda_init)
    acc = acc_ref[...] / jnp.sum(l_ref[...], axis=1, keepdims=True)
    o = acc[0:tq] - lam * acc[tq:2 * tq]
    o = _rms(o, sg_ref[...], SUBLN_EPS) * (1.0 - lambda_init)
    o_ref[...] = (o * _silu(g_ref[...].astype(F32))).astype(o_ref.dtype)


def _attn_core(qk, p, lam4, subln_g, d, lambda_init):
    lp = p.shape[0]
    heads = d // B_V_DIM
    tq = ATTN_TILE
    assert ROW0 == tq and lp % tq == 0 and lp >= 2 * tq
    kernel = functools.partial(_attn_kernel, tq=tq, lambda_init=lambda_init)
    return pl.pallas_call(
        kernel, grid=(heads, lp // tq),
        in_specs=[pl.BlockSpec((tq, B_V_DIM), lambda h, i: (i, h)),
                  pl.BlockSpec((lp, B_V_DIM), lambda h, i: (0, heads + h)),
                  pl.BlockSpec((lp, B_V_DIM), lambda h, i: (0, 2 * heads + h)),
                  pl.BlockSpec((tq, B_V_DIM), lambda h, i: (i, 3 * heads + h)),
                  pl.BlockSpec((4, B_QK_DIM), lambda h, i: (0, 0)),
                  pl.BlockSpec((1, B_V_DIM), lambda h, i: (0, 0))],
        out_specs=pl.BlockSpec((tq, B_V_DIM), lambda h, i: (i, h)),
        out_shape=jax.ShapeDtypeStruct((lp, d), BF16),
        scratch_shapes=[pltpu.VMEM((2 * tq, B_V_DIM), BF16), pltpu.VMEM((2 * tq, B_V_DIM), F32),
                        pltpu.VMEM((2 * tq, 128), F32), pltpu.VMEM((2 * tq, 128), F32)],
        compiler_params=_cparams(2), name="diff_attn",
    )(qk, qk, p, p, lam4, subln_g.reshape(1, B_V_DIM))


def _rope_tables(lp):
    pos = (jnp.arange(lp) - (ROW0 - N_META)).astype(F32)
    inv_freq = ROPE_THETA ** (-jnp.arange(0, ROPE_DIM, 2, dtype=F32) / ROPE_DIM)
    ang = pos[:, None] * inv_freq[None, :]
    cos, sin = jnp.cos(ang), jnp.sin(ang)
    pad = B_QK_DIM - ROPE_DIM
    cos_t = jnp.concatenate([cos, cos, jnp.ones((lp, pad), F32)], axis=1)
    sin_t = jnp.concatenate([-sin, sin, jnp.zeros((lp, pad), F32)], axis=1)
    return cos_t, sin_t


RWKV_GROUP = 256 // C_HEAD_DIM
RWKV_GROUP_W = RWKV_GROUP * C_HEAD_DIM


def _rwkv_kernel(r_ref, k_ref, v_ref, g_ref, dw_ref, da_ref, w0_ref, a0_ref, kk_ref, ka_ref, rk_ref,
                 lw_ref, lb_ref, o_ref, st_ref, *, groups, chunks):
    @pl.when(pl.program_id(1) == 0)
    def _():
        st_ref[...] = jnp.zeros_like(st_ref)

    n = C_HEAD_DIM
    gw = RWKV_GROUP_W
    width = groups * gw
    shift = int(math.log2(n))
    ri = _iota2((gw, gw), 0)
    ci = _iota2((gw, gw), 1)
    same = (ri >> shift) == (ci >> shift)
    seg = same.astype(BF16)

    def seg_sum(x):
        return jnp.concatenate([jnp.dot(x[:, g * gw:(g + 1) * gw].astype(BF16), seg, preferred_element_type=F32)
                                for g in range(groups)], axis=1)

    rt = ri & (n - 1)
    ct = ci & (n - 1)
    strict = same & (ct < rt)
    lower = same & (ct <= rt)
    eye = ri == ci

    def tile_rows(y):
        return jnp.concatenate([y] * RWKV_GROUP, axis=0)

    def expand(y):
        return jnp.where(same, tile_rows(y), 0.0)

    def compact(e):
        out = e[0:CHUNK]
        for h in range(1, RWKV_GROUP):
            out = out + e[h * CHUNK:(h + 1) * CHUNK]
        return out

    chains = [(c, g) for c in range(chunks) for g in range(groups)]
    tok = {}
    for c in range(chunks):
        rows = slice(c * CHUNK, (c + 1) * CHUNK)
        r = r_ref[rows, :]
        k = k_ref[rows, :]
        v = v_ref[rows, :]
        logw = -C_DECAY_SCALE * jax.nn.sigmoid(w0_ref[...] + dw_ref[rows, :])
        alpha = jax.nn.sigmoid(a0_ref[...] + da_ref[rows, :])
        kk = k * kk_ref[...]
        kk = kk / jnp.maximum(jnp.sqrt(seg_sum(kk * kk)), 1e-12)
        k2 = k * (1.0 + (alpha - 1.0) * ka_ref[...])
        cum = _chunk_cumsum(logw)
        cum_last = cum[CHUNK - 1:CHUNK, :]
        b = kk * alpha
        e_inv = jnp.exp(-cum)
        e_tail = jnp.exp(cum_last - cum)
        tok[c] = dict(r=r, v=v, k2=k2, p_t=-kk * jnp.exp(cum - logw), r_t=r * jnp.exp(cum), b_t=b * e_inv,
                      k_t=k2 * e_inv, b_h=b * e_tail, k_h=k2 * e_tail, gam=jnp.exp(cum_last))

    def part(c, g, name):
        return tok[c][name][:, g * gw:(g + 1) * gw]

    ep = {cg: expand(part(*cg, "p_t")) for cg in chains}
    ev = {cg: expand(part(*cg, "v")) for cg in chains}
    a_all = {cg: _dot_nt(jnp.concatenate([ep[cg], expand(part(*cg, "r_t"))], axis=0),
                         jnp.concatenate([tile_rows(part(*cg, "b_t")), tile_rows(part(*cg, "k_t"))], axis=0))
             for cg in chains}
    power = {cg: jnp.where(strict, a_all[cg][:gw, :gw], 0.0) for cg in chains}
    l2 = {cg: jnp.where(strict, a_all[cg][:gw, gw:], 0.0) for cg in chains}
    a3 = {cg: jnp.where(lower, a_all[cg][gw:, :gw], 0.0) for cg in chains}
    a4 = {cg: jnp.where(lower, a_all[cg][gw:, gw:], 0.0) for cg in chains}
    t_inv = {cg: jnp.where(eye, 1.0, power[cg]) for cg in chains}
    l2v = {cg: _dot(l2[cg], ev[cg]) for cg in chains}
    a4v = {cg: _dot(a4[cg], ev[cg]) for cg in chains}
    khv = {cg: _dot_tn(part(*cg, "k_h"), part(*cg, "v")) for cg in chains}
    for _ in range(int(math.log2(CHUNK)) - 1):
        power = {cg: _dot(power[cg], power[cg]) for cg in chains}
        t_inv = {cg: t_inv[cg] + _dot(t_inv[cg], power[cg]) for cg in chains}
    wv = {cg: _dot(t_inv[cg], jnp.concatenate([ep[cg], l2v[cg]], axis=1)) for cg in chains}
    qo = {cg: _dot(a3[cg], wv[cg]) for cg in chains}
    mz = {cg: _dot_tn(part(*cg, "b_h"), jnp.concatenate([compact(wv[cg][:, :gw]), compact(wv[cg][:, gw:])], axis=1))
          for cg in chains}

    for c in range(chunks):
        rows = slice(c * CHUNK, (c + 1) * CHUNK)
        outs = []
        for g in range(groups):
            cg = (c, g)
            q_t = part(c, g, "r_t") + compact(qo[cg][:, :gw])
            o_intra = compact(qo[cg][:, gw:] + a4v[cg])
            m_mat = jnp.where(same, mz[cg][:, :gw], 0.0) + jnp.where(eye, part(c, g, "gam"), 0.0)
            z_mat = jnp.where(same, mz[cg][:, gw:] + khv[cg], 0.0)
            st = st_ref[g]
            outs.append(_dot(q_t, st) + o_intra)
            st_ref[g] = _dot(m_mat, st) + z_mat
        o = jnp.concatenate(outs, axis=1)
        mean = seg_sum(o) * (1.0 / n)
        cen = o - mean
        var = seg_sum(cen * cen) * (1.0 / n)
        o = cen * lax.rsqrt(var + LNX_EPS) * lw_ref[...] + lb_ref[...]
        o = o + seg_sum(tok[c]["r"] * tok[c]["k2"] * rk_ref[...]) * tok[c]["v"]
        o_ref[rows, :] = (o * _silu(g_ref[rows, :])).astype(o_ref.dtype)


def _rwkv_core(r, k, v, gate, dw, da, vecs, d):
    lp = r.shape[0]
    groups = 2
    chunks = 4
    width = groups * RWKV_GROUP_W
    rows = chunks * CHUNK
    act = pl.BlockSpec((rows, width), lambda j, c: (c, j))
    vec = pl.BlockSpec((1, width), lambda j, c: (0, j))
    return pl.pallas_call(
        functools.partial(_rwkv_kernel, groups=groups, chunks=chunks), grid=(d // width, lp // rows),
        in_specs=[act] * 6 + [vec] * 7, out_specs=act,
        out_shape=jax.ShapeDtypeStruct((lp, d), BF16),
        scratch_shapes=[pltpu.VMEM((groups, RWKV_GROUP_W, RWKV_GROUP_W), F32)],
        compiler_params=_cparams(2), name="rwkv7",
    )(r, k, v, gate, dw, da, *[t.reshape(1, d) for t in vecs])


def _hgrn2_layer(u, w_in, j, lb, gnorm_g, w_out, d):
    p = _matmul(u, w_in, j, 4 * d)
    return _matmul(_hgrn2_core(p, lb, gnorm_g, d), w_out, j, d)


def _attn_layer(u, w_in, j, lam4, subln_g, w_out, d, lambda_init, tables):
    p = _matmul(u, w_in, j, 4 * d, out_dtype=BF16)
    qk = _rope_qk(p, *tables, d)
    return _matmul(_attn_core(qk, p, lam4, subln_g, d, lambda_init), w_out, j, d)


def _rwkv_layer(xs, w_in, j, w0, w1, w2, a0, a1, a2, k_k, k_a, r_k, lnx_w, lnx_b, w_out, d):
    r, k, v, gate = (_matmul(xs, w_in, j, d, col_off=s * d, lead=s) for s in range(4))
    dw = _lora(xs, 4, w1.astype(BF16), w2.astype(BF16), True)
    da = _lora(xs, 5, a1.astype(BF16), a2.astype(BF16), False)
    o = _rwkv_core(r, k, v, gate, dw, da, (w0, a0, k_k, k_a, r_k.reshape(-1), lnx_w, lnx_b), d)
    return _matmul(o, w_out, j, d)


def kernel(x, meta_tokens, pre_norm_g, post_norm_g, a_w_in, a_lb_logits, a_gnorm_g, a_w_out, b_w_in, b_lam_q1, b_lam_k1, b_lam_q2, b_lam_k2, b_subln_g, b_w_out, c_mu, c_w_in, c_w0, c_w1, c_w2, c_a0, c_a1, c_a2, c_k_k, c_k_a, c_r_k, c_lnx_w, c_lnx_b, c_w_out):
    bsz, seq, d = x.shape
    depth = pre_norm_g.shape[0]
    assert bsz == 1 and seq % ATTN_TILE == 0 and meta_tokens.shape[0] == N_META and depth >= 2
    lp = ROW0 + seq
    h = (jnp.concatenate([jnp.zeros((ROW0 - N_META, d), F32), meta_tokens.astype(F32)], axis=0), x[0])
    tables = _rope_tables(lp)
    lb_p = jax.nn.softmax(a_lb_logits.astype(F32), axis=0)
    lb_all = jnp.cumsum(lb_p, axis=0) - lb_p[0]

    u = _prenorm(h, pre_norm_g[0], BF16)
    for i in range(depth):
        kind, j = i % N_MIXERS, i // N_MIXERS
        if kind == 0:
            y = _hgrn2_layer(u, a_w_in, j, lb_all[j], a_gnorm_g[j], a_w_out, d)
        elif kind == 1:
            lambda_init = 0.8 - 0.6 * math.exp(-0.3 * i)
            lam4 = jnp.stack([b_lam_q1[j], b_lam_k1[j], b_lam_q2[j], b_lam_k2[j]]).astype(F32)
            y = _attn_layer(u, b_w_in, j, lam4, b_subln_g[j], b_w_out, d, lambda_init, tables)
        else:
            y = _rwkv_layer(u, c_w_in, j, c_w0[j], c_w1[j], c_w2[j], c_a0[j], c_a1[j], c_a2[j],
                            c_k_k[j], c_k_a[j], c_r_k[j], c_lnx_w[j], c_lnx_b[j], c_w_out, d)
        if i + 1 == depth:
            out = _residual_out(h[0], y, post_norm_g[i], seq)
        elif (i + 1) % N_MIXERS == 2:
            hn, u = _residual_shiftmix(h[0], y, post_norm_g[i], pre_norm_g[i + 1], c_mu[(i + 1) // N_MIXERS])
            h = (hn,)
        else:
            hn, u = _residual_prenorm(h, y, post_norm_g[i], pre_norm_g[i + 1], BF16)
            h = (hn,)
    return out[None]
```

```python
import functools
import math

import jax
import jax.numpy as jnp
import numpy as np
from jax import lax
from jax.experimental import pallas as pl
from jax.experimental.pallas import tpu as pltpu

F32 = jnp.float32
BF16 = jnp.bfloat16

CHUNK = 64
N_META = 16
N_MIXERS = 3
NORM_EPS = 1e-6
A_HEAD_DIM = 128
B_QK_DIM = 128
B_V_DIM = 2 * B_QK_DIM
ROPE_THETA = 500000.0
ROPE_DIM = B_QK_DIM // 4
SUBLN_EPS = 1e-5
C_HEAD_DIM = 64
C_DECAY_SCALE = 0.606531
LNX_EPS = 64e-5

ROW0 = 512
ROW_TILE = 256
ATTN_TILE = 512
MASK_VALUE = -1e30
NO_CHUNK = 1 << 30
TILE_UNROLL = 4
Q_SCALE_LOG2 =B_QK_DIM ** -0.5 * math.log2(math.e)
V7X_VMEM_LIMIT = 56 * 1024 * 1024


def _cparams(n_axes, vmem=V7X_VMEM_LIMIT):
    return pltpu.CompilerParams(dimension_semantics=("arbitrary",) * n_axes, vmem_limit_bytes=vmem)


def _dot(a, b):
    return jnp.dot(a.astype(BF16), b.astype(BF16), preferred_element_type=F32)


def _dot_nt(a, b):
    return lax.dot_general(a.astype(BF16), b.astype(BF16), (((1,), (1,)), ((), ())), preferred_element_type=F32)


def _dot_tn(a, b):
    return lax.dot_general(a.astype(BF16), b.astype(BF16), (((0,), (0,)), ((), ())), preferred_element_type=F32)


def _iota2(shape, axis):
    return lax.broadcasted_iota(jnp.int32, shape, axis)


def _chunk_cumsum(x):
    tri = (_iota2((CHUNK, CHUNK), 1) <= _iota2((CHUNK, CHUNK), 0)).astype(BF16)
    hi = x.astype(BF16)
    lo = (x - hi.astype(F32)).astype(BF16)
    return jnp.dot(tri, hi, preferred_element_type=F32) + jnp.dot(tri, lo, preferred_element_type=F32)


def _silu(x):
    return x * jax.nn.sigmoid(x)


def _rms(x, g, eps):
    return x * lax.rsqrt(jnp.mean(x * x, axis=-1, keepdims=True) + eps) * g


HEAD_TILES = ROW0 // ROW_TILE


def _stream_rows(refs):
    if len(refs) == 1:
        return refs[0][...]
    head_ref, frames_ref = refs
    return jnp.where(pl.program_id(0) < HEAD_TILES, head_ref[...], frames_ref[...])


def _stream_specs(h):
    d = h[0].shape[1]
    if len(h) == 1:
        return [pl.BlockSpec((ROW_TILE, d), lambda i: (i, 0))]
    return [pl.BlockSpec((ROW_TILE, d), lambda i: (jnp.minimum(i, HEAD_TILES - 1), 0)),
            pl.BlockSpec((ROW_TILE, d), lambda i: (jnp.maximum(i - HEAD_TILES, 0), 0))]


def _prenorm_kernel(*refs, n_h):
    g_ref, u_ref = refs[n_h:]
    u_ref[...] = _rms(_stream_rows(refs[:n_h]), g_ref[...], NORM_EPS).astype(u_ref.dtype)


def _residual_prenorm_kernel(*refs, n_h):
    y_ref, gp_ref, g_ref, hn_ref, u_ref = refs[n_h:]
    hn = _stream_rows(refs[:n_h]) + _rms(y_ref[...], gp_ref[...], NORM_EPS)
    hn_ref[...] = hn
    u_ref[...] = _rms(hn, g_ref[...], NORM_EPS).astype(u_ref.dtype)


def _residual_shiftmix_kernel(h_ref, y_ref, hp_ref, yp_ref, gp_ref, g_ref, mu_ref, hn_ref, xs_ref):
    hn = h_ref[...] + _rms(y_ref[...], gp_ref[...], NORM_EPS)
    hn_ref[...] = hn
    u = _rms(hn, g_ref[...], NORM_EPS)
    last = _rms(hp_ref[7:8, :] + _rms(yp_ref[7:8, :], gp_ref[...], NORM_EPS), g_ref[...], NORM_EPS)
    last = jnp.where(pl.program_id(0) == 0, jnp.zeros_like(last), last)
    prev = jnp.where(_iota2(u.shape, 0) == 0, last, pltpu.roll(u, 1, 0))
    delta = prev - u
    for s in range(xs_ref.shape[0]):
        xs_ref[s] = (u + delta * mu_ref[s:s + 1, :]).astype(xs_ref.dtype)


def _residual_kernel(h_ref, y_ref, gp_ref, o_ref):
    o_ref[...] = h_ref[...] + _rms(y_ref[...], gp_ref[...], NORM_EPS)


def _prenorm(h, g, u_dtype):
    d = g.shape[0]
    lp = h[0].shape[0] if len(h) == 1 else ROW0 + h[1].shape[0]
    row = pl.BlockSpec((ROW_TILE, d), lambda i: (i, 0))
    vec = pl.BlockSpec((1, d), lambda i: (0, 0))
    return pl.pallas_call(
        functools.partial(_prenorm_kernel, n_h=len(h)), grid=(lp // ROW_TILE,),
        in_specs=_stream_specs(h) + [vec], out_specs=row,
        out_shape=jax.ShapeDtypeStruct((lp, d), u_dtype), compiler_params=_cparams(1), name="prenorm",
    )(*h, g.reshape(1, d))


def _residual_prenorm(h, y, gp, g, u_dtype):
    lp, d = y.shape
    row = pl.BlockSpec((ROW_TILE, d), lambda i: (i, 0))
    vec = pl.BlockSpec((1, d), lambda i: (0, 0))
    return pl.pallas_call(
        functools.partial(_residual_prenorm_kernel, n_h=len(h)), grid=(lp // ROW_TILE,),
        in_specs=_stream_specs(h) + [row, vec, vec], out_specs=[row, row],
        out_shape=[jax.ShapeDtypeStruct((lp, d), F32), jax.ShapeDtypeStruct((lp, d), u_dtype)],
        compiler_params=_cparams(1), name="residual_prenorm",
    )(*h, y, gp.reshape(1, d), g.reshape(1, d))


def _residual_shiftmix(h, y, gp, g, mu):
    lp, d = y.shape
    n = mu.shape[0]
    tile = ROW_TILE // 2
    per8 = tile // 8
    row = pl.BlockSpec((tile, d), lambda i: (i, 0))
    before = pl.BlockSpec((8, d), lambda i: (jnp.maximum(i * per8 - 1, 0), 0))
    vec = pl.BlockSpec((1, d), lambda i: (0, 0))
    return pl.pallas_call(
        _residual_shiftmix_kernel, grid=(lp // tile,),
        in_specs=[row, row, before, before, vec, vec, pl.BlockSpec((n, d), lambda i: (0, 0))],
        out_specs=[row, pl.BlockSpec((n, tile, d), lambda i: (0, i, 0))],
        out_shape=[jax.ShapeDtypeStruct((lp, d), F32), jax.ShapeDtypeStruct((n, lp, d), BF16)],
        compiler_params=_cparams(1), name="residual_shiftmix",
    )(h, y, h, y, gp.reshape(1, d), g.reshape(1, d), mu)


def _residual_out(h, y, gp, seq):
    lp, d = h.shape
    off = ROW0 // ROW_TILE
    row_in = pl.BlockSpec((ROW_TILE, d), lambda i: (i + off, 0))
    vec = pl.BlockSpec((1, d), lambda i: (0, 0))
    return pl.pallas_call(
        _residual_kernel, grid=(seq // ROW_TILE,), in_specs=[row_in, row_in, vec],
        out_specs=pl.BlockSpec((ROW_TILE, d), lambda i: (i, 0)),
        out_shape=jax.ShapeDtypeStruct((seq, d), F32), compiler_params=_cparams(1), name="residual_out",
    )(h, y, gp.reshape(1, d))


def _mm_kernel(a_ref, w_ref, o_ref):
    o_ref[...] = jnp.dot(a_ref[...], w_ref[...].astype(BF16), preferred_element_type=F32).astype(o_ref.dtype)


def _row_block(m, cap=1536):
    n = m // ROW_TILE
    best = max(d for d in range(1, n + 1) if n % d == 0 and ROW_TILE * d <= cap)
    return ROW_TILE * best


def _lhs_spec(a, tm, lead, index_map):
    k = a.shape[-1]
    if a.ndim == 2:
        return pl.BlockSpec((tm, k), lambda *g: (index_map(*g), 0))
    return pl.BlockSpec((None, tm, k), lambda *g: (lead, index_map(*g), 0))


def _matmul(a, w, layer, n_out, col_off=0, lead=0, out_dtype=F32, tn=512):
    m = a.shape[-2]
    k = a.shape[-1]
    tm = _row_block(m)
    tn = min(tn, n_out)
    joff = col_off // tn
    return pl.pallas_call(
        _mm_kernel, grid=(m // tm, n_out // tn),
        in_specs=[_lhs_spec(a, tm, lead, lambda i, j: i),
                  pl.BlockSpec((None, k, tn), lambda i, j: (layer, 0, j + joff))],
        out_specs=pl.BlockSpec((tm, tn), lambda i, j: (i, j)),
        out_shape=jax.ShapeDtypeStruct((m, n_out), out_dtype), compiler_params=_cparams(2), name="matmul",
    )(a, w)


def _rope(x, cos, sin):
    first = _iota2(x.shape, 1) < ROPE_DIM // 2
    other = jnp.where(first, pltpu.roll(x, B_QK_DIM - ROPE_DIM // 2, 1), pltpu.roll(x, ROPE_DIM // 2, 1))
    return x * cos + other * sin


def _rope_kernel(p_ref, cos_ref, sin_ref, o_ref):
    fac = jnp.where(pl.program_id(1) == 0, Q_SCALE_LOG2, 1.0)
    cos = cos_ref[...] * fac
    sin = sin_ref[...] * fac
    for c in range(p_ref.shape[1] // B_QK_DIM):
        cols = slice(c * B_QK_DIM, (c + 1) * B_QK_DIM)
        o_ref[:, cols] = _rope(p_ref[:, cols].astype(F32), cos, sin).astype(o_ref.dtype)


def _rope_qk(p, cos_t, sin_t, d):
    lp = p.shape[0]
    table = pl.BlockSpec((ROW_TILE, B_QK_DIM), lambda r, c: (r, 0))
    return pl.pallas_call(
        _rope_kernel, grid=(lp // ROW_TILE, 2),
        in_specs=[pl.BlockSpec((ROW_TILE, d), lambda r, c: (r, c)), table, table],
        out_specs=pl.BlockSpec((ROW_TILE, d), lambda r, c: (r, c)),
        out_shape=jax.ShapeDtypeStruct((lp, 2 * d), BF16), compiler_params=_cparams(2), name="rope_qk",
    )(p, cos_t, sin_t)


def _lora_kernel(x_ref, a_ref, b_ref, o_ref, *, use_tanh):
    mid = jnp.dot(x_ref[...], a_ref[...], preferred_element_type=F32)
    if use_tanh:
        mid = jnp.tanh(mid)
    o_ref[...] = _dot(mid, b_ref[...])


def _lora(x, lead, a, b, use_tanh):
    m, d = x.shape[-2:]
    r = a.shape[1]
    return pl.pallas_call(
        functools.partial(_lora_kernel, use_tanh=use_tanh), grid=(m // ROW_TILE,),
        in_specs=[_lhs_spec(x, ROW_TILE, lead, lambda i: i), pl.BlockSpec((d, r), lambda i: (0, 0)),
                  pl.BlockSpec((r, d), lambda i: (0, 0))],
        out_specs=pl.BlockSpec((ROW_TILE, d), lambda i: (i, 0)),
        out_shape=jax.ShapeDtypeStruct((m, d), F32), compiler_params=_cparams(1), name="lora",
    )(x, a, b)


HGRN2_LEVELS = tuple(CHUNK >> (i + 1) for i in range(int(math.log2(CHUNK))))


def _hgrn2_range_sums():
    t = np.arange(CHUNK)[:, None]
    c = np.arange(CHUNK)[None, :]
    blocks = [c <= t]
    for h in HGRN2_LEVELS:
        r = (t // (2 * h)) * 2 * h + h
        blocks.append(np.where(t >= r, (c > r) & (c <= t), (c > t) & (c <= r)))
    return jnp.asarray(np.concatenate(blocks, axis=0), dtype=BF16)


def _hgrn2_kernel(q_ref, f_ref, i_ref, g_ref, lb_ref, gn_ref, rs_ref, o_ref, st_ref, *, heads, chunks):
    @pl.when(pl.program_id(1) == 0)
    def _():
        st_ref[...] = jnp.zeros_like(st_ref)

    dh = A_HEAD_DIM
    width = heads * dh
    n_lev = len(HGRN2_LEVELS)
    ti = _iota2((CHUNK, CHUNK), 0)
    si = _iota2((CHUNK, CHUNK), 1)
    eye = ti == si
    pair_masks = []
    for h in HGRN2_LEVELS:
        sh = int(math.log2(h))
        pair_masks.append((((ti >> sh) & 1) == 1) & ((ti >> (sh + 1)) == (si >> (sh + 1))) & (((si >> sh) & 1) == 0))
    trow = _iota2((CHUNK, width), 0)
    upper = [((trow >> int(math.log2(h))) & 1) == 1 for h in HGRN2_LEVELS]
    lb = lb_ref[...]
    head_cols = [slice(h * dh, (h + 1) * dh) for h in range(heads)]

    tok = []
    for c in range(chunks):
        rows = slice(c * CHUNK, (c + 1) * CHUNK)
        q = q_ref[rows, :]
        v = i_ref[rows, :]
        f = lb + (1.0 - lb) * jax.nn.sigmoid(f_ref[rows, :])
        g = jnp.log(f)
        k = 1.0 - f
        g_hi = g.astype(BF16)
        g_lo = (g - g_hi.astype(F32)).astype(BF16)
        e2 = jnp.dot(rs_ref[...], jnp.concatenate([g_hi, g_lo], axis=1), preferred_element_type=F32)
        e = e2[:, :width] + e2[:, width:]
        b = e[0:CHUNK]
        b_last = b[CHUNK - 1:CHUNK, :]
        ex = jnp.exp(e)
        xs = [jnp.where(upper[i], q, k) * ex[(i + 1) * CHUNK:(i + 2) * CHUNK] for i in range(n_lev)]
        tok.append(dict(q=q, k=k, v=v, qe=q * ex[0:CHUNK], kd=k * jnp.exp(b_last - b), decay=jnp.exp(b_last), xs=xs))

    pairs = [(c, cs) for c in range(chunks) for cs in head_cols]
    scores = [jnp.where(eye, _dot_nt(tok[c]["q"][:, cs], tok[c]["k"][:, cs]), 0.0) for c, cs in pairs]
    for i in range(n_lev):
        scores = [sc + jnp.where(pair_masks[i], _dot_nt(tok[c]["xs"][i][:, cs], tok[c]["xs"][i][:, cs]), 0.0)
                  for sc, (c, cs) in zip(scores, pairs)]
    intra = [_dot(sc, tok[c]["v"][:, cs]) for sc, (c, cs) in zip(scores, pairs)]
    kv = [_dot_tn(tok[c]["v"][:, cs], tok[c]["kd"][:, cs]) for c, cs in pairs]
    for n, (c, cs) in enumerate(pairs):
        h = n % heads
        rows = slice(c * CHUNK, (c + 1) * CHUNK)
        st = st_ref[h]
        o = _dot_nt(tok[c]["qe"][:, cs], st) + intra[n]
        st_ref[h] = st * tok[c]["decay"][:, cs] + kv[n]
        o = _rms(o, gn_ref[...], NORM_EPS)
        o_ref[rows, cs] = (o * _silu(g_ref[rows, cs])).astype(o_ref.dtype)


def _hgrn2_core(p, lb, gnorm_g, d):
    lp = p.shape[0]
    heads = 4
    width = heads * A_HEAD_DIM
    ncol = d // width
    blk = lambda part: pl.BlockSpec((ROW_TILE, width), lambda j, r: (r, part * ncol + j))
    range_sums = _hgrn2_range_sums()
    return pl.pallas_call(
        functools.partial(_hgrn2_kernel, heads=heads, chunks=ROW_TILE // CHUNK),
        grid=(ncol, lp // ROW_TILE),
        in_specs=[blk(0), blk(1), blk(2), blk(3), pl.BlockSpec((1, width), lambda j, r: (0, j)),
                  pl.BlockSpec((1, A_HEAD_DIM), lambda j, r: (0, 0)),
                  pl.BlockSpec(range_sums.shape, lambda j, r: (0, 0))],
        out_specs=pl.BlockSpec((ROW_TILE, width), lambda j, r: (r, j)),
        out_shape=jax.ShapeDtypeStruct((lp, d), BF16),
        scratch_shapes=[pltpu.VMEM((heads, A_HEAD_DIM, A_HEAD_DIM), F32)],
        compiler_params=_cparams(2), name="hgrn2",
    )(p, p, p, p, lb.reshape(1, d), gnorm_g.reshape(1, A_HEAD_DIM), range_sums)


def _row_chunk_id(row):
    return jnp.maximum((row - (ROW0 - CHUNK)) >> 6, 0)


def _attn_kernel(q_ref, k_ref, v_ref, g_ref, lam_ref, sg_ref, o_ref, qx_ref, acc_ref, m_ref, l_ref, *,
                 tq, lambda_init):
    i = pl.program_id(1)
    q = q_ref[...]
    lane = _iota2(q.shape, 1)
    qx_ref[0:tq, :] = jnp.where(lane < B_QK_DIM, q, jnp.zeros_like(q))
    qx_ref[tq:2 * tq, :] = jnp.where(lane >= B_QK_DIM, q, jnp.zeros_like(q))
    acc_ref[...] = jnp.zeros_like(acc_ref)
    m_ref[...] = jnp.full_like(m_ref, MASK_VALUE)
    l_ref[...] = jnp.zeros_like(l_ref)

    def lane_fold(x, op):
        out = x[:, 0:128]
        for t in range(1, x.shape[1] // 128):
            out = op(out, x[:, t * 128:(t + 1) * 128])
        return out

    def lane_tile(x, n):
        return jnp.concatenate([x] * n, axis=1)

    def key_rows(row0, width):
        return pl.ds(row0 if isinstance(row0, int) else pl.multiple_of(row0, tq), width)

    def scores(row0, width):
        rows = key_rows(row0, width)
        return lax.dot_general(qx_ref[...], k_ref[rows, :], (((1,), (1,)), ((), ())), preferred_element_type=F32)

    q_chunk = _row_chunk_id(i * tq + (_iota2((2 * tq, 1), 0) & (tq - 1)))

    def mask_scores(s, row0):
        k_row = row0 + _iota2((1, s.shape[1]), 1)
        k_chunk = jnp.where(k_row >= ROW0 - N_META, _row_chunk_id(k_row), NO_CHUNK)
        return jnp.where(k_chunk <= q_chunk, s, MASK_VALUE)

    def soften(s):
        m_prev = m_ref[...]
        m_new = jnp.maximum(m_prev, jnp.max(lane_fold(s, jnp.maximum), axis=1, keepdims=True))
        alpha = jnp.exp2(m_prev - m_new)
        p = jnp.exp2(s - lane_tile(m_new, s.shape[1] // 128))
        l_ref[...] = alpha * l_ref[...] + lane_fold(p, jnp.add)
        m_ref[...] = m_new
        return p.astype(BF16), alpha

    def value_product(p, row0, width):
        return jnp.dot(p, v_ref[key_rows(row0, width), :], preferred_element_type=F32)

    def rescale_add(alpha, pv):
        acc_ref[...] = acc_ref[...] * lane_tile(alpha, B_V_DIM // 128) + pv

    first_key = (ROW0 - N_META) // 128 * 128
    head_width = tq - first_key

    def closing_step(s_last):
        row0 = jnp.maximum(i, 1) * tq
        p, alpha = soften(jnp.concatenate(
            [mask_scores(scores(first_key, head_width), first_key), mask_scores(s_last, row0)], axis=1))
        rescale_add(alpha, value_product(p[:, :head_width], first_key, head_width)
                    + value_product(p[:, head_width:], row0, tq))

    n_full = jnp.maximum(i - 1, 0)

    def tile_start(t):
        return tq * (t + 1)

    def full_tile(t, s_cur):
        s_next = scores(tile_start(t + 1), tq)
        p, alpha = soften(s_cur)
        rescale_add(alpha, value_product(p, tq * (t + 1), tq))
        return s_next

    def unrolled_tiles(u, s_cur):
        for t in range(TILE_UNROLL):
            s_cur = full_tile(TILE_UNROLL * u + t, s_cur)
        return s_cur

    n_main = n_full // TILE_UNROLL
    s_tail = lax.fori_loop(0, n_main, unrolled_tiles, scores(tile_start(0), tq))
    s_diag = lax.fori_loop(n_main * TILE_UNROLL, n_full, full_tile, s_tail)

    closing_step(s_diag)

    lam4 = lam_ref[...]
    lam = (jnp.exp(jnp.sum(lam4[0:1] * lam4[1:2], axis=1, keepdims=True))
           - jnp.exp(jnp.sum(lam4[2:3] * lam4[3:4], axis=1, keepdims=True)) + lambda_init)
    acc = acc_ref[...] / jnp.sum(l_ref[...], axis=1, keepdims=True)
    o = acc[0:tq] - lam * acc[tq:2 * tq]
    o = _rms(o, sg_ref[...], SUBLN_EPS) * (1.0 - lambda_init)
    o_ref[...] = (o * _silu(g_ref[...].astype(F32))).astype(o_ref.dtype)


def _attn_core(qk, p, lam4, subln_g, d, lambda_init):
    lp = p.shape[0]
    heads = d // B_V_DIM
    tq = ATTN_TILE
    assert ROW0 == tq and lp % tq == 0 and lp >= 2 * tq
    kernel = functools.partial(_attn_kernel, tq=tq, lambda_init=lambda_init)
    return pl.pallas_call(
        kernel, grid=(heads, lp // tq),
        in_specs=[pl.BlockSpec((tq, B_V_DIM), lambda h, i: (i, h)),
                  pl.BlockSpec((lp, B_V_DIM), lambda h, i: (0, heads + h)),
                  pl.BlockSpec((lp, B_V_DIM), lambda h, i: (0, 2 * heads + h)),
                  pl.BlockSpec((tq, B_V_DIM), lambda h, i: (i, 3 * heads + h)),
                  pl.BlockSpec((4, B_QK_DIM), lambda h, i: (0, 0)),
                  pl.BlockSpec((1, B_V_DIM), lambda h, i: (0, 0))],
        out_specs=pl.BlockSpec((tq, B_V_DIM), lambda h, i: (i, h)),
        out_shape=jax.ShapeDtypeStruct((lp, d), BF16),
        scratch_shapes=[pltpu.VMEM((2 * tq, B_V_DIM), BF16), pltpu.VMEM((2 * tq, B_V_DIM), F32),
                        pltpu.VMEM((2 * tq, 128), F32), pltpu.VMEM((2 * tq, 128), F32)],
        compiler_params=_cparams(2), name="diff_attn",
    )(qk, qk, p, p, lam4, subln_g.reshape(1, B_V_DIM))


def _rope_tables(lp):
    pos = (jnp.arange(lp) - (ROW0 - N_META)).astype(F32)
    inv_freq = ROPE_THETA ** (-jnp.arange(0, ROPE_DIM, 2, dtype=F32) / ROPE_DIM)
    ang = pos[:, None] * inv_freq[None, :]
    cos, sin = jnp.cos(ang), jnp.sin(ang)
    pad = B_QK_DIM - ROPE_DIM
    cos_t = jnp.concatenate([cos, cos, jnp.ones((lp, pad), F32)], axis=1)
    sin_t = jnp.concatenate([-sin, sin, jnp.zeros((lp, pad), F32)], axis=1)
    return cos_t, sin_t


RWKV_GROUP = 256 // C_HEAD_DIM
RWKV_GROUP_W = RWKV_GROUP * C_HEAD_DIM
RWKV_SKEW = 1


def _rwkv_kernel(r_ref, k_ref, v_ref, g_ref, dw_ref, da_ref, w0_ref, a0_ref, kk_ref, ka_ref, rk_ref,
                 lw_ref, lb_ref, o_ref, st_ref, *, groups, chunks):
    @pl.when(pl.program_id(1) == 0)
    def _():
        st_ref[...] = jnp.zeros_like(st_ref)

    n = C_HEAD_DIM
    gw = RWKV_GROUP_W
    width = groups * gw
    shift = int(math.log2(n))
    ri = _iota2((gw, gw), 0)
    ci = _iota2((gw, gw), 1)
    same = (ri >> shift) == (ci >> shift)
    seg = same.astype(BF16)

    def seg_sum(x):
        return jnp.concatenate([jnp.dot(x[:, g * gw:(g + 1) * gw].astype(BF16), seg, preferred_element_type=F32)
                                for g in range(groups)], axis=1)

    strict = same & ((ci & (n - 1)) < (ri & (n - 1)))
    eye = ri == ci
    wide_t = _iota2((CHUNK, gw), 0)
    wide_s = _iota2((CHUNK, gw), 1) & (n - 1)
    strict_w = wide_s < wide_t
    lower_w = wide_s <= wide_t

    def tile_rows(y):
        return jnp.concatenate([y] * RWKV_GROUP, axis=0)

    def expand(y):
        return jnp.where(same, tile_rows(y), 0.0)

    def compact(e):
        out = e[0:CHUNK]
        for h in range(1, RWKV_GROUP):
            out = out + e[h * CHUNK:(h + 1) * CHUNK]
        return out

    tok = {}
    per = {}

    def part(c, g, name):
        return tok[c][name][:, g * gw:(g + 1) * gw]

    def prepare(c):
        rows = slice(c * CHUNK, (c + 1) * CHUNK)
        r = r_ref[rows, :]
        k = k_ref[rows, :]
        v = v_ref[rows, :]
        logw = -C_DECAY_SCALE * jax.nn.sigmoid(w0_ref[...] + dw_ref[rows, :])
        alpha = jax.nn.sigmoid(a0_ref[...] + da_ref[rows, :])
        kk = k * kk_ref[...]
        kk = kk / jnp.maximum(jnp.sqrt(seg_sum(kk * kk)), 1e-12)
        k2 = k * (1.0 + (alpha - 1.0) * ka_ref[...])
        cum = _chunk_cumsum(logw)
        cum_last = cum[CHUNK - 1:CHUNK, :]
        b = kk * alpha
        e_inv = jnp.exp(-cum)
        e_tail = jnp.exp(cum_last - cum)
        tok[c] = dict(r=r, v=v, k2=k2, p_t=-kk * jnp.exp(cum - logw), r_t=r * jnp.exp(cum), b_t=b * e_inv,
                      k_t=k2 * e_inv, b_h=b * e_tail, k_h=k2 * e_tail, gam=jnp.exp(cum_last))
        for g in range(groups):
            x = per[c, g] = dict(ep=expand(part(c, g, "p_t")), ev=expand(part(c, g, "v")))
            x["a_all"] = _dot_nt(jnp.concatenate([part(c, g, "p_t"), part(c, g, "r_t")], axis=0),
                                 jnp.concatenate([expand(part(c, g, "b_t")), expand(part(c, g, "k_t"))], axis=0))

    def split(c):
        for g in range(groups):
            x = per[c, g]
            a_all = x.pop("a_all")
            x["power"] = jnp.where(strict, tile_rows(a_all[:CHUNK, :gw]), 0.0)
            x["t_inv"] = jnp.where(eye, 1.0, x["power"])
            x["a3"] = jnp.where(lower_w, a_all[CHUNK:, :gw], 0.0)
            x["l2v"] = expand(_dot(jnp.where(strict_w, a_all[:CHUNK, gw:], 0.0), x["ev"]))
            x["a4v"] = _dot(jnp.where(lower_w, a_all[CHUNK:, gw:], 0.0), x["ev"])

    def neumann(c):
        for g in range(groups):
            x = per[c, g]
            x["power"] = _dot(x["power"], x["power"])
            x["t_inv"] = x["t_inv"] + _dot(x["t_inv"], x["power"])

    def solve(c):
        for g in range(groups):
            x = per[c, g]
            x["wv"] = _dot(x["t_inv"], jnp.concatenate([x["ep"], x["l2v"]], axis=1))

    def finish(c):
        for g in range(groups):
            x = per[c, g]
            qo = _dot(x["a3"], x["wv"])
            x["q_t"] = part(c, g, "r_t") + qo[:, :gw]
            x["o_intra"] = qo[:, gw:] + x["a4v"]
            x["w_c"] = compact(x["wv"][:, :gw])
            x["v_c"] = compact(x["wv"][:, gw:])
            x["bk"] = jnp.concatenate([part(c, g, "b_h"), part(c, g, "k_h")], axis=0)
            x["gam_col"] = jnp.sum(jnp.where(eye, part(c, g, "gam"), 0.0), axis=1, keepdims=True)

    def read_state(c):
        for g in range(groups):
            x = per[c, g]
            st = st_ref[g]
            x["o"] = _dot(x["q_t"], st) + x["o_intra"]
            x["u_c"] = _dot(x["w_c"], st) + x["v_c"]
            x["st_decayed"] = st * x["gam_col"]

    def write_state(c):
        rows = slice(c * CHUNK, (c + 1) * CHUNK)
        for g in range(groups):
            x = per[c, g]
            update = _dot_tn(x["bk"], jnp.concatenate([x["u_c"], part(c, g, "v")], axis=0))
            st_ref[g] = x["st_decayed"] + jnp.where(same, update, 0.0)
        o = jnp.concatenate([per[c, g]["o"] for g in range(groups)], axis=1)
        mean = seg_sum(o) * (1.0 / n)
        cen = o - mean
        var = seg_sum(cen * cen) * (1.0 / n)
        o = cen * lax.rsqrt(var + LNX_EPS) * lw_ref[...] + lb_ref[...]
        o = o + seg_sum(tok[c]["r"] * tok[c]["k2"] * rk_ref[...]) * tok[c]["v"]
        o_ref[rows, :] = (o * _silu(g_ref[rows, :])).astype(o_ref.dtype)

    stages = [prepare, split] + [neumann] * (int(math.log2(CHUNK)) - 1) + [solve, finish, read_state, write_state]
    for tick in range(len(stages) + RWKV_SKEW * (chunks - 1)):
        for c in range(chunks):
            s = tick - RWKV_SKEW * c
            if 0 <= s < len(stages):
                stages[s](c)


def _rwkv_core(r, k, v, gate, dw, da, vecs, d):
    lp = r.shape[0]
    groups = min(4, d // RWKV_GROUP_W)
    chunks = 4
    width = groups * RWKV_GROUP_W
    rows = chunks * CHUNK
    act = pl.BlockSpec((rows, width), lambda j, c: (c, j))
    vec = pl.BlockSpec((1, width), lambda j, c: (0, j))
    return pl.pallas_call(
        functools.partial(_rwkv_kernel, groups=groups, chunks=chunks), grid=(d // width, lp // rows),
        in_specs=[act] * 6 + [vec] * 7, out_specs=act,
        out_shape=jax.ShapeDtypeStruct((lp, d), BF16),
        scratch_shapes=[pltpu.VMEM((groups, RWKV_GROUP_W, RWKV_GROUP_W), F32)],
        compiler_params=_cparams(2), name="rwkv7",
    )(r, k, v, gate, dw, da, *[t.reshape(1, d) for t in vecs])


def _hgrn2_layer(u, w_in, j, lb, gnorm_g, w_out, d):
    p = _matmul(u, w_in, j, 4 * d)
    return _matmul(_hgrn2_core(p, lb, gnorm_g, d), w_out, j, d)


def _attn_layer(u, w_in, j, lam4, subln_g, w_out, d, lambda_init, tables):
    p = _matmul(u, w_in, j, 4 * d, out_dtype=BF16)
    qk = _rope_qk(p, *tables, d)
    return _matmul(_attn_core(qk, p, lam4, subln_g, d, lambda_init), w_out, j, d)


def _rwkv_layer(xs, w_in, j, w0, w1, w2, a0, a1, a2, k_k, k_a, r_k, lnx_w, lnx_b, w_out, d):
    r, k, v, gate = (_matmul(xs, w_in, j, d, col_off=s * d, lead=s) for s in range(4))
    dw = _lora(xs, 4, w1.astype(BF16), w2.astype(BF16), True)
    da = _lora(xs, 5, a1.astype(BF16), a2.astype(BF16), False)
    o = _rwkv_core(r, k, v, gate, dw, da, (w0, a0, k_k, k_a, r_k.reshape(-1), lnx_w, lnx_b), d)
    return _matmul(o, w_out, j, d)


def kernel(x, meta_tokens, pre_norm_g, post_norm_g, a_w_in, a_lb_logits, a_gnorm_g, a_w_out, b_w_in, b_lam_q1, b_lam_k1, b_lam_q2, b_lam_k2, b_subln_g, b_w_out, c_mu, c_w_in, c_w0, c_w1, c_w2, c_a0, c_a1, c_a2, c_k_k, c_k_a, c_r_k, c_lnx_w, c_lnx_b, c_w_out):
    bsz, seq, d = x.shape
    depth = pre_norm_g.shape[0]
    assert bsz == 1 and seq % ATTN_TILE == 0 and meta_tokens.shape[0] == N_META and depth >= 2
    lp = ROW0 + seq
    h = (jnp.concatenate([jnp.zeros((ROW0 - N_META, d), F32), meta_tokens.astype(F32)], axis=0), x[0])
    tables = _rope_tables(lp)
    lb_p = jax.nn.softmax(a_lb_logits.astype(F32), axis=0)
    lb_all = jnp.cumsum(lb_p, axis=0) - lb_p[0]

    u = _prenorm(h, pre_norm_g[0], BF16)
    for i in range(depth):
        kind, j = i % N_MIXERS, i // N_MIXERS
        if kind == 0:
            y = _hgrn2_layer(u, a_w_in, j, lb_all[j], a_gnorm_g[j], a_w_out, d)
        elif kind == 1:
            lambda_init = 0.8 - 0.6 * math.exp(-0.3 * i)
            lam4 = jnp.stack([b_lam_q1[j], b_lam_k1[j], b_lam_q2[j], b_lam_k2[j]]).astype(F32)
            y = _attn_layer(u, b_w_in, j, lam4, b_subln_g[j], b_w_out, d, lambda_init, tables)
        else:
            y = _rwkv_layer(u, c_w_in, j, c_w0[j], c_w1[j], c_w2[j], c_a0[j], c_a1[j], c_a2[j],
                            c_k_k[j], c_k_a[j], c_r_k[j], c_lnx_w[j], c_lnx_b[j], c_w_out, d)
        if i + 1 == depth:
            out = _residual_out(h[0], y, post_norm_g[i], seq)
        elif (i + 1) % N_MIXERS == 2:
            hn, u = _residual_shiftmix(h[0], y, post_norm_g[i], pre_norm_g[i + 1], c_mu[(i + 1) // N_MIXERS])
            h = (hn,)
        else:
            hn, u = _residual_prenorm(h, y, post_norm_g[i], pre_norm_g[i + 1], BF16)
            h = (hn,)
    return out[None]
```

```python
import functools
import math

import jax
import jax.numpy as jnp
import numpy as np
from jax import lax
from jax.experimental import pallas as pl
from jax.experimental.pallas import tpu as pltpu

F32 = jnp.float32
BF16 = jnp.bfloat16

CHUNK = 64
N_META = 16
N_MIXERS = 3
NORM_EPS = 1e-6
A_HEAD_DIM = 128
B_QK_DIM = 128
B_V_DIM = 2 * B_QK_DIM
ROPE_THETA = 500000.0
ROPE_DIM = B_QK_DIM // 4
SUBLN_EPS = 1e-5
C_HEAD_DIM = 64
C_DECAY_SCALE = 0.606531
LNX_EPS = 64e-5

ROW0 = 512
ROW_TILE = 256
ATTN_TILE = 512
MASK_VALUE = -1e30
NO_CHUNK = 1 << 30
TILE_UNROLL = 4
Q_SCALE_LOG2 =B_QK_DIM ** -0.5 * math.log2(math.e)
V7X_VMEM_LIMIT = 56 * 1024 * 1024


def _cparams(n_axes, vmem=V7X_VMEM_LIMIT):
    return pltpu.CompilerParams(dimension_semantics=("arbitrary",) * n_axes, vmem_limit_bytes=vmem)


def _dot(a, b):
    return jnp.dot(a.astype(BF16), b.astype(BF16), preferred_element_type=F32)


def _dot_nt(a, b):
    return lax.dot_general(a.astype(BF16), b.astype(BF16), (((1,), (1,)), ((), ())), preferred_element_type=F32)


def _dot_tn(a, b):
    return lax.dot_general(a.astype(BF16), b.astype(BF16), (((0,), (0,)), ((), ())), preferred_element_type=F32)


def _iota2(shape, axis):
    return lax.broadcasted_iota(jnp.int32, shape, axis)


def _chunk_cumsum(x):
    tri = (_iota2((CHUNK, CHUNK), 1) <= _iota2((CHUNK, CHUNK), 0)).astype(BF16)
    hi = x.astype(BF16)
    lo = (x - hi.astype(F32)).astype(BF16)
    return jnp.dot(tri, hi, preferred_element_type=F32) + jnp.dot(tri, lo, preferred_element_type=F32)


def _silu(x):
    return x * jax.nn.sigmoid(x)


def _rms(x, g, eps):
    return x * lax.rsqrt(jnp.mean(x * x, axis=-1, keepdims=True) + eps) * g


HEAD_TILES = ROW0 // ROW_TILE


def _stream_rows(refs):
    if len(refs) == 1:
        return refs[0][...]
    head_ref, frames_ref = refs
    return jnp.where(pl.program_id(0) < HEAD_TILES, head_ref[...], frames_ref[...])


def _stream_specs(h):
    d = h[0].shape[1]
    if len(h) == 1:
        return [pl.BlockSpec((ROW_TILE, d), lambda i: (i, 0))]
    return [pl.BlockSpec((ROW_TILE, d), lambda i: (jnp.minimum(i, HEAD_TILES - 1), 0)),
            pl.BlockSpec((ROW_TILE, d), lambda i: (jnp.maximum(i - HEAD_TILES, 0), 0))]


def _prenorm_kernel(*refs, n_h):
    g_ref, u_ref = refs[n_h:]
    u_ref[...] = _rms(_stream_rows(refs[:n_h]), g_ref[...], NORM_EPS).astype(u_ref.dtype)


def _residual_prenorm_kernel(*refs, n_h):
    y_ref, gp_ref, g_ref, hn_ref, u_ref = refs[n_h:]
    hn = _stream_rows(refs[:n_h]) + _rms(y_ref[...], gp_ref[...], NORM_EPS)
    hn_ref[...] = hn
    u_ref[...] = _rms(hn, g_ref[...], NORM_EPS).astype(u_ref.dtype)


def _residual_shiftmix_kernel(h_ref, y_ref, hp_ref, yp_ref, gp_ref, g_ref, mu_ref, hn_ref, xs_ref):
    hn = h_ref[...] + _rms(y_ref[...], gp_ref[...], NORM_EPS)
    hn_ref[...] = hn
    u = _rms(hn, g_ref[...], NORM_EPS)
    last = _rms(hp_ref[7:8, :] + _rms(yp_ref[7:8, :], gp_ref[...], NORM_EPS), g_ref[...], NORM_EPS)
    last = jnp.where(pl.program_id(0) == 0, jnp.zeros_like(last), last)
    prev = jnp.where(_iota2(u.shape, 0) == 0, last, pltpu.roll(u, 1, 0))
    delta = prev - u
    for s in range(xs_ref.shape[0]):
        xs_ref[s] = (u + delta * mu_ref[s:s + 1, :]).astype(xs_ref.dtype)


def _residual_kernel(h_ref, y_ref, gp_ref, o_ref):
    o_ref[...] = h_ref[...] + _rms(y_ref[...], gp_ref[...], NORM_EPS)


def _prenorm(h, g, u_dtype):
    d = g.shape[0]
    lp = h[0].shape[0] if len(h) == 1 else ROW0 + h[1].shape[0]
    row = pl.BlockSpec((ROW_TILE, d), lambda i: (i, 0))
    vec = pl.BlockSpec((1, d), lambda i: (0, 0))
    return pl.pallas_call(
        functools.partial(_prenorm_kernel, n_h=len(h)), grid=(lp // ROW_TILE,),
        in_specs=_stream_specs(h) + [vec], out_specs=row,
        out_shape=jax.ShapeDtypeStruct((lp, d), u_dtype), compiler_params=_cparams(1), name="prenorm",
    )(*h, g.reshape(1, d))


def _residual_prenorm(h, y, gp, g, u_dtype):
    lp, d = y.shape
    row = pl.BlockSpec((ROW_TILE, d), lambda i: (i, 0))
    vec = pl.BlockSpec((1, d), lambda i: (0, 0))
    return pl.pallas_call(
        functools.partial(_residual_prenorm_kernel, n_h=len(h)), grid=(lp // ROW_TILE,),
        in_specs=_stream_specs(h) + [row, vec, vec], out_specs=[row, row],
        out_shape=[jax.ShapeDtypeStruct((lp, d), F32), jax.ShapeDtypeStruct((lp, d), u_dtype)],
        compiler_params=_cparams(1), name="residual_prenorm",
    )(*h, y, gp.reshape(1, d), g.reshape(1, d))


def _residual_shiftmix(h, y, gp, g, mu):
    lp, d = y.shape
    n = mu.shape[0]
    tile = ROW_TILE // 2
    per8 = tile // 8
    row = pl.BlockSpec((tile, d), lambda i: (i, 0))
    before = pl.BlockSpec((8, d), lambda i: (jnp.maximum(i * per8 - 1, 0), 0))
    vec = pl.BlockSpec((1, d), lambda i: (0, 0))
    return pl.pallas_call(
        _residual_shiftmix_kernel, grid=(lp // tile,),
        in_specs=[row, row, before, before, vec, vec, pl.BlockSpec((n, d), lambda i: (0, 0))],
        out_specs=[row, pl.BlockSpec((n, tile, d), lambda i: (0, i, 0))],
        out_shape=[jax.ShapeDtypeStruct((lp, d), F32), jax.ShapeDtypeStruct((n, lp, d), BF16)],
        compiler_params=_cparams(1), name="residual_shiftmix",
    )(h, y, h, y, gp.reshape(1, d), g.reshape(1, d), mu)


def _residual_out(h, y, gp, seq):
    lp, d = h.shape
    off = ROW0 // ROW_TILE
    row_in = pl.BlockSpec((ROW_TILE, d), lambda i: (i + off, 0))
    vec = pl.BlockSpec((1, d), lambda i: (0, 0))
    return pl.pallas_call(
        _residual_kernel, grid=(seq // ROW_TILE,), in_specs=[row_in, row_in, vec],
        out_specs=pl.BlockSpec((ROW_TILE, d), lambda i: (i, 0)),
        out_shape=jax.ShapeDtypeStruct((seq, d), F32), compiler_params=_cparams(1), name="residual_out",
    )(h, y, gp.reshape(1, d))


def _mm_kernel(a_ref, w_ref, o_ref):
    o_ref[...] = jnp.dot(a_ref[...], w_ref[...].astype(BF16), preferred_element_type=F32).astype(o_ref.dtype)


def _row_block(m, cap=1536):
    n = m // ROW_TILE
    best = max(d for d in range(1, n + 1) if n % d == 0 and ROW_TILE * d <= cap)
    return ROW_TILE * best


def _lhs_spec(a, tm, lead, index_map):
    k = a.shape[-1]
    if a.ndim == 2:
        return pl.BlockSpec((tm, k), lambda *g: (index_map(*g), 0))
    return pl.BlockSpec((None, tm, k), lambda *g: (lead, index_map(*g), 0))


def _matmul(a, w, layer, n_out, col_off=0, lead=0, out_dtype=F32, tn=512):
    m = a.shape[-2]
    k = a.shape[-1]
    tm = _row_block(m)
    tn = min(tn, n_out)
    joff = col_off // tn
    return pl.pallas_call(
        _mm_kernel, grid=(m // tm, n_out // tn),
        in_specs=[_lhs_spec(a, tm, lead, lambda i, j: i),
                  pl.BlockSpec((None, k, tn), lambda i, j: (layer, 0, j + joff))],
        out_specs=pl.BlockSpec((tm, tn), lambda i, j: (i, j)),
        out_shape=jax.ShapeDtypeStruct((m, n_out), out_dtype), compiler_params=_cparams(2), name="matmul",
    )(a, w)


def _rope(x, cos, sin):
    first = _iota2(x.shape, 1) < ROPE_DIM // 2
    other = jnp.where(first, pltpu.roll(x, B_QK_DIM - ROPE_DIM // 2, 1), pltpu.roll(x, ROPE_DIM // 2, 1))
    return x * cos + other * sin


def _rope_kernel(p_ref, cos_ref, sin_ref, o_ref):
    fac = jnp.where(pl.program_id(1) == 0, Q_SCALE_LOG2, 1.0)
    cos = cos_ref[...] * fac
    sin = sin_ref[...] * fac
    for c in range(p_ref.shape[1] // B_QK_DIM):
        cols = slice(c * B_QK_DIM, (c + 1) * B_QK_DIM)
        o_ref[:, cols] = _rope(p_ref[:, cols].astype(F32), cos, sin).astype(o_ref.dtype)


def _rope_qk(p, cos_t, sin_t, d):
    lp = p.shape[0]
    table = pl.BlockSpec((ROW_TILE, B_QK_DIM), lambda r, c: (r, 0))
    return pl.pallas_call(
        _rope_kernel, grid=(lp // ROW_TILE, 2),
        in_specs=[pl.BlockSpec((ROW_TILE, d), lambda r, c: (r, c)), table, table],
        out_specs=pl.BlockSpec((ROW_TILE, d), lambda r, c: (r, c)),
        out_shape=jax.ShapeDtypeStruct((lp, 2 * d), BF16), compiler_params=_cparams(2), name="rope_qk",
    )(p, cos_t, sin_t)


def _lora_kernel(x_ref, a_ref, b_ref, o_ref, *, use_tanh):
    mid = jnp.dot(x_ref[...], a_ref[...], preferred_element_type=F32)
    if use_tanh:
        mid = jnp.tanh(mid)
    o_ref[...] = _dot(mid, b_ref[...])


def _lora(x, lead, a, b, use_tanh):
    m, d = x.shape[-2:]
    r = a.shape[1]
    return pl.pallas_call(
        functools.partial(_lora_kernel, use_tanh=use_tanh), grid=(m // ROW_TILE,),
        in_specs=[_lhs_spec(x, ROW_TILE, lead, lambda i: i), pl.BlockSpec((d, r), lambda i: (0, 0)),
                  pl.BlockSpec((r, d), lambda i: (0, 0))],
        out_specs=pl.BlockSpec((ROW_TILE, d), lambda i: (i, 0)),
        out_shape=jax.ShapeDtypeStruct((m, d), F32), compiler_params=_cparams(1), name="lora",
    )(x, a, b)


HGRN2_LEVELS = tuple(CHUNK >> (i + 1) for i in range(int(math.log2(CHUNK))))


def _hgrn2_range_sums():
    t = np.arange(CHUNK)[:, None]
    c = np.arange(CHUNK)[None, :]
    blocks = [c <= t]
    for h in HGRN2_LEVELS:
        r = (t // (2 * h)) * 2 * h + h
        blocks.append(np.where(t >= r, (c > r) & (c <= t), (c > t) & (c <= r)))
    return jnp.asarray(np.concatenate(blocks, axis=0), dtype=BF16)


def _hgrn2_kernel(q_ref, f_ref, i_ref, g_ref, lb_ref, gn_ref, rs_ref, o_ref, st_ref, *, heads, chunks):
    @pl.when(pl.program_id(1) == 0)
    def _():
        st_ref[...] = jnp.zeros_like(st_ref)

    dh = A_HEAD_DIM
    width = heads * dh
    n_lev = len(HGRN2_LEVELS)
    ti = _iota2((CHUNK, CHUNK), 0)
    si = _iota2((CHUNK, CHUNK), 1)
    eye = ti == si
    pair_masks = []
    for h in HGRN2_LEVELS:
        sh = int(math.log2(h))
        pair_masks.append((((ti >> sh) & 1) == 1) & ((ti >> (sh + 1)) == (si >> (sh + 1))) & (((si >> sh) & 1) == 0))
    trow = _iota2((CHUNK, width), 0)
    upper = [((trow >> int(math.log2(h))) & 1) == 1 for h in HGRN2_LEVELS]
    lb = lb_ref[...]
    head_cols = [slice(h * dh, (h + 1) * dh) for h in range(heads)]

    tok = []
    for c in range(chunks):
        rows = slice(c * CHUNK, (c + 1) * CHUNK)
        q = q_ref[rows, :]
        v = i_ref[rows, :]
        f = lb + (1.0 - lb) * jax.nn.sigmoid(f_ref[rows, :])
        g = jnp.log(f)
        k = 1.0 - f
        g_hi = g.astype(BF16)
        g_lo = (g - g_hi.astype(F32)).astype(BF16)
        e2 = jnp.dot(rs_ref[...], jnp.concatenate([g_hi, g_lo], axis=1), preferred_element_type=F32)
        e = e2[:, :width] + e2[:, width:]
        b = e[0:CHUNK]
        b_last = b[CHUNK - 1:CHUNK, :]
        ex = jnp.exp(e)
        xs = [jnp.where(upper[i], q, k) * ex[(i + 1) * CHUNK:(i + 2) * CHUNK] for i in range(n_lev)]
        tok.append(dict(q=q, k=k, v=v, qe=q * ex[0:CHUNK], kd=k * jnp.exp(b_last - b), decay=jnp.exp(b_last), xs=xs))

    pairs = [(c, cs) for c in range(chunks) for cs in head_cols]
    scores = [jnp.where(eye, _dot_nt(tok[c]["q"][:, cs], tok[c]["k"][:, cs]), 0.0) for c, cs in pairs]
    for i in range(n_lev):
        scores = [sc + jnp.where(pair_masks[i], _dot_nt(tok[c]["xs"][i][:, cs], tok[c]["xs"][i][:, cs]), 0.0)
                  for sc, (c, cs) in zip(scores, pairs)]
    intra = [_dot(sc, tok[c]["v"][:, cs]) for sc, (c, cs) in zip(scores, pairs)]
    kv = [_dot_tn(tok[c]["v"][:, cs], tok[c]["kd"][:, cs]) for c, cs in pairs]
    for n, (c, cs) in enumerate(pairs):
        h = n % heads
        rows = slice(c * CHUNK, (c + 1) * CHUNK)
        st = st_ref[h]
        o = _dot_nt(tok[c]["qe"][:, cs], st) + intra[n]
        st_ref[h] = st * tok[c]["decay"][:, cs] + kv[n]
        o = _rms(o, gn_ref[...], NORM_EPS)
        o_ref[rows, cs] = (o * _silu(g_ref[rows, cs])).astype(o_ref.dtype)


def _hgrn2_core(p, lb, gnorm_g, d):
    lp = p.shape[0]
    heads = 4
    width = heads * A_HEAD_DIM
    ncol = d // width
    blk = lambda part: pl.BlockSpec((ROW_TILE, width), lambda j, r: (r, part * ncol + j))
    range_sums = _hgrn2_range_sums()
    return pl.pallas_call(
        functools.partial(_hgrn2_kernel, heads=heads, chunks=ROW_TILE // CHUNK),
        grid=(ncol, lp // ROW_TILE),
        in_specs=[blk(0), blk(1), blk(2), blk(3), pl.BlockSpec((1, width), lambda j, r: (0, j)),
                  pl.BlockSpec((1, A_HEAD_DIM), lambda j, r: (0, 0)),
                  pl.BlockSpec(range_sums.shape, lambda j, r: (0, 0))],
        out_specs=pl.BlockSpec((ROW_TILE, width), lambda j, r: (r, j)),
        out_shape=jax.ShapeDtypeStruct((lp, d), BF16),
        scratch_shapes=[pltpu.VMEM((heads, A_HEAD_DIM, A_HEAD_DIM), F32)],
        compiler_params=_cparams(2), name="hgrn2",
    )(p, p, p, p, lb.reshape(1, d), gnorm_g.reshape(1, A_HEAD_DIM), range_sums)


def _row_chunk_id(row):
    return jnp.maximum((row - (ROW0 - CHUNK)) >> 6, 0)


def _attn_kernel(q_ref, k_ref, v_ref, g_ref, lam_ref, sg_ref, o_ref, qx_ref, s_ref, acc_ref, m_ref, l_ref, *,
                 tq, lambda_init):
    i = pl.program_id(1)
    q = q_ref[...]
    lane = _iota2(q.shape, 1)
    qx_ref[0:tq, :] = jnp.where(lane < B_QK_DIM, q, jnp.zeros_like(q))
    qx_ref[tq:2 * tq, :] = jnp.where(lane >= B_QK_DIM, q, jnp.zeros_like(q))
    acc_ref[...] = jnp.zeros_like(acc_ref)
    m_ref[...] = jnp.full_like(m_ref, MASK_VALUE)
    l_ref[...] = jnp.zeros_like(l_ref)

    def lane_fold(x, op):
        out = x[:, 0:128]
        for t in range(1, x.shape[1] // 128):
            out = op(out, x[:, t * 128:(t + 1) * 128])
        return out

    def lane_tile(x, n):
        return jnp.concatenate([x] * n, axis=1)

    def key_rows(row0, width):
        return pl.ds(row0 if isinstance(row0, int) else pl.multiple_of(row0, tq), width)

    def scores(row0, width):
        rows = key_rows(row0, width)
        return lax.dot_general(qx_ref[...], k_ref[rows, :], (((1,), (1,)), ((), ())), preferred_element_type=F32)

    q_chunk = _row_chunk_id(i * tq + (_iota2((2 * tq, 1), 0) & (tq - 1)))

    def mask_scores(s, row0):
        k_row = row0 + _iota2((1, s.shape[1]), 1)
        k_chunk = jnp.where(k_row >= ROW0 - N_META, _row_chunk_id(k_row), NO_CHUNK)
        return jnp.where(k_chunk <= q_chunk, s, MASK_VALUE)

    def soften(s):
        m_prev = m_ref[...]
        m_new = jnp.maximum(m_prev, jnp.max(lane_fold(s, jnp.maximum), axis=1, keepdims=True))
        alpha = jnp.exp2(m_prev - m_new)
        p = jnp.exp2(s - lane_tile(m_new, s.shape[1] // 128))
        l_ref[...] = alpha * l_ref[...] + lane_fold(p, jnp.add)
        m_ref[...] = m_new
        return p.astype(BF16), alpha

    def value_product(p, row0, width):
        return jnp.dot(p, v_ref[key_rows(row0, width), :], preferred_element_type=F32)

    def rescale_add(alpha, pv):
        acc_ref[...] = acc_ref[...] * lane_tile(alpha, B_V_DIM // 128) + pv

    first_key = (ROW0 - N_META) // 128 * 128
    head_width = tq - first_key

    def closing_step(s_last):
        row0 = jnp.maximum(i, 1) * tq
        p, alpha = soften(jnp.concatenate(
            [mask_scores(scores(first_key, head_width), first_key), mask_scores(s_last, row0)], axis=1))
        rescale_add(alpha, value_product(p[:, :head_width], first_key, head_width)
                    + value_product(p[:, head_width:], row0, tq))

    n_full = jnp.maximum(i - 1, 0)

    def tile_start(t):
        return tq * (t + 1)

    def unrolled_tiles(u, carry):
        for t in range(TILE_UNROLL):
            cur = t % 2
            s_ref[1 - cur] = scores(tile_start(TILE_UNROLL * u + t + 1), tq)
            p, alpha = soften(s_ref[cur])
            rescale_add(alpha, value_product(p, tq * (TILE_UNROLL * u + t + 1), tq))
        return carry

    def single_tile(t, carry):
        s_next = scores(tile_start(t + 1), tq)
        p, alpha = soften(s_ref[0])
        rescale_add(alpha, value_product(p, tq * (t + 1), tq))
        s_ref[0] = s_next
        return carry

    n_main = n_full // TILE_UNROLL
    s_ref[0] = scores(tile_start(0), tq)
    lax.fori_loop(0, n_main, unrolled_tiles, 0)
    lax.fori_loop(n_main * TILE_UNROLL, n_full, single_tile, 0)

    closing_step(s_ref[0])

    lam4 = lam_ref[...]
    lam = (jnp.exp(jnp.sum(lam4[0:1] * lam4[1:2], axis=1, keepdims=True))
           - jnp.exp(jnp.sum(lam4[2:3] * lam4[3:4], axis=1, keepdims=True)) + lambda_init)
    acc = acc_ref[...] / jnp.sum(l_ref[...], axis=1, keepdims=True)
    o = acc[0:tq] - lam * acc[tq:2 * tq]
    o = _rms(o, sg_ref[...], SUBLN_EPS) * (1.0 - lambda_init)
    o_ref[...] = (o * _silu(g_ref[...].astype(F32))).astype(o_ref.dtype)


def _attn_core(qk, p, lam4, subln_g, d, lambda_init):
    lp = p.shape[0]
    heads = d // B_V_DIM
    tq = ATTN_TILE
    assert ROW0 == tq and lp % tq == 0 and lp >= 2 * tq
    kernel = functools.partial(_attn_kernel, tq=tq, lambda_init=lambda_init)
    return pl.pallas_call(
        kernel, grid=(heads, lp // tq),
        in_specs=[pl.BlockSpec((tq, B_V_DIM), lambda h, i: (i, h)),
                  pl.BlockSpec((lp, B_V_DIM), lambda h, i: (0, heads + h)),
                  pl.BlockSpec((lp, B_V_DIM), lambda h, i: (0, 2 * heads + h)),
                  pl.BlockSpec((tq, B_V_DIM), lambda h, i: (i, 3 * heads + h)),
                  pl.BlockSpec((4, B_QK_DIM), lambda h, i: (0, 0)),
                  pl.BlockSpec((1, B_V_DIM), lambda h, i: (0, 0))],
        out_specs=pl.BlockSpec((tq, B_V_DIM), lambda h, i: (i, h)),
        out_shape=jax.ShapeDtypeStruct((lp, d), BF16),
        scratch_shapes=[pltpu.VMEM((2 * tq, B_V_DIM), BF16), pltpu.VMEM((2, 2 * tq, tq), F32),
                        pltpu.VMEM((2 * tq, B_V_DIM), F32),
                        pltpu.VMEM((2 * tq, 128), F32), pltpu.VMEM((2 * tq, 128), F32)],
        compiler_params=_cparams(2), name="diff_attn",
    )(qk, qk, p, p, lam4, subln_g.reshape(1, B_V_DIM))


def _rope_tables(lp):
    pos = (jnp.arange(lp) - (ROW0 - N_META)).astype(F32)
    inv_freq = ROPE_THETA ** (-jnp.arange(0, ROPE_DIM, 2, dtype=F32) / ROPE_DIM)
    ang = pos[:, None] * inv_freq[None, :]
    cos, sin = jnp.cos(ang), jnp.sin(ang)
    pad = B_QK_DIM - ROPE_DIM
    cos_t = jnp.concatenate([cos, cos, jnp.ones((lp, pad), F32)], axis=1)
    sin_t = jnp.concatenate([-sin, sin, jnp.zeros((lp, pad), F32)], axis=1)
    return cos_t, sin_t


RWKV_GROUP = 256 // C_HEAD_DIM
RWKV_GROUP_W = RWKV_GROUP * C_HEAD_DIM
RWKV_SKEW = 1


def _rwkv_kernel(r_ref, k_ref, v_ref, g_ref, dw_ref, da_ref, w0_ref, a0_ref, kk_ref, ka_ref, rk_ref,
                 lw_ref, lb_ref, o_ref, st_ref, *, groups, chunks):
    @pl.when(pl.program_id(1) == 0)
    def _():
        st_ref[...] = jnp.zeros_like(st_ref)

    n = C_HEAD_DIM
    gw = RWKV_GROUP_W
    width = groups * gw
    shift = int(math.log2(n))
    ri = _iota2((gw, gw), 0)
    ci = _iota2((gw, gw), 1)
    same = (ri >> shift) == (ci >> shift)
    seg = same.astype(BF16)

    def seg_sum(x):
        return jnp.concatenate([jnp.dot(x[:, g * gw:(g + 1) * gw].astype(BF16), seg, preferred_element_type=F32)
                                for g in range(groups)], axis=1)

    strict = same & ((ci & (n - 1)) < (ri & (n - 1)))
    eye = ri == ci
    wide_t = _iota2((CHUNK, gw), 0)
    wide_s = _iota2((CHUNK, gw), 1) & (n - 1)
    strict_w = wide_s < wide_t
    lower_w = wide_s <= wide_t

    def tile_rows(y):
        return jnp.concatenate([y] * RWKV_GROUP, axis=0)

    def expand(y):
        return jnp.where(same, tile_rows(y), 0.0)

    def compact(e):
        out = e[0:CHUNK]
        for h in range(1, RWKV_GROUP):
            out = out + e[h * CHUNK:(h + 1) * CHUNK]
        return out

    tok = {}
    per = {}

    def part(c, g, name):
        return tok[c][name][:, g * gw:(g + 1) * gw]

    def prepare(c):
        rows = slice(c * CHUNK, (c + 1) * CHUNK)
        r = r_ref[rows, :]
        k = k_ref[rows, :]
        v = v_ref[rows, :]
        logw = -C_DECAY_SCALE * jax.nn.sigmoid(w0_ref[...] + dw_ref[rows, :])
        alpha = jax.nn.sigmoid(a0_ref[...] + da_ref[rows, :])
        kk = k * kk_ref[...]
        kk = kk / jnp.maximum(jnp.sqrt(seg_sum(kk * kk)), 1e-12)
        k2 = k * (1.0 + (alpha - 1.0) * ka_ref[...])
        cum = _chunk_cumsum(logw)
        cum_last = cum[CHUNK - 1:CHUNK, :]
        b = kk * alpha
        e_inv = jnp.exp(-cum)
        e_tail = jnp.exp(cum_last - cum)
        tok[c] = dict(r=r, v=v, k2=k2, p_t=-kk * jnp.exp(cum - logw), r_t=r * jnp.exp(cum), b_t=b * e_inv,
                      k_t=k2 * e_inv, b_h=b * e_tail, k_h=k2 * e_tail, gam=jnp.exp(cum_last))
        for g in range(groups):
            x = per[c, g] = dict(ep=expand(part(c, g, "p_t")), ev=expand(part(c, g, "v")))
            x["a_all"] = _dot_nt(jnp.concatenate([part(c, g, "p_t"), part(c, g, "r_t")], axis=0),
                                 jnp.concatenate([expand(part(c, g, "b_t")), expand(part(c, g, "k_t"))], axis=0))

    def split(c):
        for g in range(groups):
            x = per[c, g]
            a_all = x.pop("a_all")
            x["power"] = jnp.where(strict, tile_rows(a_all[:CHUNK, :gw]), 0.0)
            x["t_inv"] = jnp.where(eye, 1.0, x["power"])
            x["a3"] = jnp.where(lower_w, a_all[CHUNK:, :gw], 0.0)
            x["l2v"] = expand(_dot(jnp.where(strict_w, a_all[:CHUNK, gw:], 0.0), x["ev"]))
            x["a4v"] = _dot(jnp.where(lower_w, a_all[CHUNK:, gw:], 0.0), x["ev"])

    def neumann(c):
        for g in range(groups):
            x = per[c, g]
            x["power"] = _dot(x["power"], x["power"])
            x["t_inv"] = x["t_inv"] + _dot(x["t_inv"], x["power"])

    def solve(c):
        for g in range(groups):
            x = per[c, g]
            x["wv"] = _dot(x["t_inv"], jnp.concatenate([x["ep"], x["l2v"]], axis=1))

    def finish(c):
        for g in range(groups):
            x = per[c, g]
            qo = _dot(x["a3"], x["wv"])
            x["q_t"] = part(c, g, "r_t") + qo[:, :gw]
            x["o_intra"] = qo[:, gw:] + x["a4v"]
            x["w_c"] = compact(x["wv"][:, :gw])
            x["v_c"] = compact(x["wv"][:, gw:])
            x["bk"] = jnp.concatenate([part(c, g, "b_h"), part(c, g, "k_h")], axis=0)
            x["gam_col"] = jnp.sum(jnp.where(eye, part(c, g, "gam"), 0.0), axis=1, keepdims=True)

    def read_state(c):
        for g in range(groups):
            x = per[c, g]
            st = st_ref[g]
            x["o"] = _dot(x["q_t"], st) + x["o_intra"]
            x["u_c"] = _dot(x["w_c"], st) + x["v_c"]
            x["st_decayed"] = st * x["gam_col"]

    def write_state(c):
        rows = slice(c * CHUNK, (c + 1) * CHUNK)
        for g in range(groups):
            x = per[c, g]
            update = _dot_tn(x["bk"], jnp.concatenate([x["u_c"], part(c, g, "v")], axis=0))
            st_ref[g] = x["st_decayed"] + jnp.where(same, update, 0.0)
        o = jnp.concatenate([per[c, g]["o"] for g in range(groups)], axis=1)
        mean = seg_sum(o) * (1.0 / n)
        cen = o - mean
        var = seg_sum(cen * cen) * (1.0 / n)
        o = cen * lax.rsqrt(var + LNX_EPS) * lw_ref[...] + lb_ref[...]
        o = o + seg_sum(tok[c]["r"] * tok[c]["k2"] * rk_ref[...]) * tok[c]["v"]
        o_ref[rows, :] = (o * _silu(g_ref[rows, :])).astype(o_ref.dtype)

    stages = [prepare, split] + [neumann] * (int(math.log2(CHUNK)) - 1) + [solve, finish, read_state, write_state]
    for tick in range(len(stages) + RWKV_SKEW * (chunks - 1)):
        for c in range(chunks):
            s = tick - RWKV_SKEW * c
            if 0 <= s < len(stages):
                stages[s](c)


def _rwkv_core(r, k, v, gate, dw, da, vecs, d):
    lp = r.shape[0]
    groups = min(4, d // RWKV_GROUP_W)
    chunks = 4
    width = groups * RWKV_GROUP_W
    rows = chunks * CHUNK
    act = pl.BlockSpec((rows, width), lambda j, c: (c, j))
    vec = pl.BlockSpec((1, width), lambda j, c: (0, j))
    return pl.pallas_call(
        functools.partial(_rwkv_kernel, groups=groups, chunks=chunks), grid=(d // width, lp // rows),
        in_specs=[act] * 6 + [vec] * 7, out_specs=act,
        out_shape=jax.ShapeDtypeStruct((lp, d), BF16),
        scratch_shapes=[pltpu.VMEM((groups, RWKV_GROUP_W, RWKV_GROUP_W), F32)],
        compiler_params=_cparams(2), name="rwkv7",
    )(r, k, v, gate, dw, da, *[t.reshape(1, d) for t in vecs])


def _hgrn2_layer(u, w_in, j, lb, gnorm_g, w_out, d):
    p = _matmul(u, w_in, j, 4 * d)
    return _matmul(_hgrn2_core(p, lb, gnorm_g, d), w_out, j, d)


def _attn_layer(u, w_in, j, lam4, subln_g, w_out, d, lambda_init, tables):
    p = _matmul(u, w_in, j, 4 * d, out_dtype=BF16)
    qk = _rope_qk(p, *tables, d)
    return _matmul(_attn_core(qk, p, lam4, subln_g, d, lambda_init), w_out, j, d)


def _rwkv_layer(xs, w_in, j, w0, w1, w2, a0, a1, a2, k_k, k_a, r_k, lnx_w, lnx_b, w_out, d):
    r, k, v, gate = (_matmul(xs, w_in, j, d, col_off=s * d, lead=s) for s in range(4))
    dw = _lora(xs, 4, w1.astype(BF16), w2.astype(BF16), True)
    da = _lora(xs, 5, a1.astype(BF16), a2.astype(BF16), False)
    o = _rwkv_core(r, k, v, gate, dw, da, (w0, a0, k_k, k_a, r_k.reshape(-1), lnx_w, lnx_b), d)
    return _matmul(o, w_out, j, d)


def kernel(x, meta_tokens, pre_norm_g, post_norm_g, a_w_in, a_lb_logits, a_gnorm_g, a_w_out, b_w_in, b_lam_q1, b_lam_k1, b_lam_q2, b_lam_k2, b_subln_g, b_w_out, c_mu, c_w_in, c_w0, c_w1, c_w2, c_a0, c_a1, c_a2, c_k_k, c_k_a, c_r_k, c_lnx_w, c_lnx_b, c_w_out):
    bsz, seq, d = x.shape
    depth = pre_norm_g.shape[0]
    assert bsz == 1 and seq % ATTN_TILE == 0 and meta_tokens.shape[0] == N_META and depth >= 2
    lp = ROW0 + seq
    h = (jnp.concatenate([jnp.zeros((ROW0 - N_META, d), F32), meta_tokens.astype(F32)], axis=0), x[0])
    tables = _rope_tables(lp)
    lb_p = jax.nn.softmax(a_lb_logits.astype(F32), axis=0)
    lb_all = jnp.cumsum(lb_p, axis=0) - lb_p[0]

    u = _prenorm(h, pre_norm_g[0], BF16)
    for i in range(depth):
        kind, j = i % N_MIXERS, i // N_MIXERS
        if kind == 0:
            y = _hgrn2_layer(u, a_w_in, j, lb_all[j], a_gnorm_g[j], a_w_out, d)
        elif kind == 1:
            lambda_init = 0.8 - 0.6 * math.exp(-0.3 * i)
            lam4 = jnp.stack([b_lam_q1[j], b_lam_k1[j], b_lam_q2[j], b_lam_k2[j]]).astype(F32)
            y = _attn_layer(u, b_w_in, j, lam4, b_subln_g[j], b_w_out, d, lambda_init, tables)
        else:
            y = _rwkv_layer(u, c_w_in, j, c_w0[j], c_w1[j], c_w2[j], c_a0[j], c_a1[j], c_a2[j],
                            c_k_k[j], c_k_a[j], c_r_k[j], c_lnx_w[j], c_lnx_b[j], c_w_out, d)
        if i + 1 == depth:
            out = _residual_out(h[0], y, post_norm_g[i], seq)
        elif (i + 1) % N_MIXERS == 2:
            hn, u = _residual_shiftmix(h[0], y, post_norm_g[i], pre_norm_g[i + 1], c_mu[(i + 1) // N_MIXERS])
            h = (hn,)
        else:
            hn, u = _residual_prenorm(h, y, post_norm_g[i], pre_norm_g[i + 1], BF16)
            h = (hn,)
    return out[None]
```

```python
import functools
import math

import jax
import jax.numpy as jnp
import numpy as np
from jax import lax
from jax.experimental import pallas as pl
from jax.experimental.pallas import tpu as pltpu

F32 = jnp.float32
BF16 = jnp.bfloat16

CHUNK = 64
N_META = 16
N_MIXERS = 3
NORM_EPS = 1e-6
A_HEAD_DIM = 128
B_QK_DIM = 128
B_V_DIM = 2 * B_QK_DIM
ROPE_THETA = 500000.0
ROPE_DIM = B_QK_DIM // 4
SUBLN_EPS = 1e-5
C_HEAD_DIM = 64
C_DECAY_SCALE = 0.606531
LNX_EPS = 64e-5

ROW0 = 512
ROW_TILE = 256
ATTN_TILE = 512
MASK_VALUE = -1e30
NO_CHUNK = 1 << 30
TILE_UNROLL = 4
Q_SCALE_LOG2 = B_QK_DIM ** -0.5 * math.log2(math.e)
V7X_VMEM_LIMIT = 56 * 1024 * 1024


def _cparams(n_axes, vmem=V7X_VMEM_LIMIT):
    return pltpu.CompilerParams(dimension_semantics=("arbitrary",) * n_axes, vmem_limit_bytes=vmem)


def _dot(a, b):
    return jnp.dot(a.astype(BF16), b.astype(BF16), preferred_element_type=F32)


def _dot_nt(a, b):
    return lax.dot_general(a.astype(BF16), b.astype(BF16), (((1,), (1,)), ((), ())), preferred_element_type=F32)


def _dot_tn(a, b):
    return lax.dot_general(a.astype(BF16), b.astype(BF16), (((0,), (0,)), ((), ())), preferred_element_type=F32)


def _iota2(shape, axis):
    return lax.broadcasted_iota(jnp.int32, shape, axis)


def _chunk_cumsum(x):
    tri = (_iota2((CHUNK, CHUNK), 1) <= _iota2((CHUNK, CHUNK), 0)).astype(BF16)
    hi = x.astype(BF16)
    lo = (x - hi.astype(F32)).astype(BF16)
    return jnp.dot(tri, hi, preferred_element_type=F32) + jnp.dot(tri, lo, preferred_element_type=F32)


def _silu(x):
    return x * jax.nn.sigmoid(x)


def _rms(x, g, eps):
    return x * lax.rsqrt(jnp.mean(x * x, axis=-1, keepdims=True) + eps) * g


HEAD_TILES = ROW0 // ROW_TILE


def _stream_rows(refs):
    if len(refs) == 1:
        return refs[0][...]
    head_ref, frames_ref = refs
    return jnp.where(pl.program_id(0) < HEAD_TILES, head_ref[...], frames_ref[...])


def _stream_specs(h):
    d = h[0].shape[1]
    if len(h) == 1:
        return [pl.BlockSpec((ROW_TILE, d), lambda i: (i, 0))]
    return [pl.BlockSpec((ROW_TILE, d), lambda i: (jnp.minimum(i, HEAD_TILES - 1), 0)),
            pl.BlockSpec((ROW_TILE, d), lambda i: (jnp.maximum(i - HEAD_TILES, 0), 0))]


def _prenorm_kernel(*refs, n_h):
    g_ref, u_ref = refs[n_h:]
    u_ref[...] = _rms(_stream_rows(refs[:n_h]), g_ref[...], NORM_EPS).astype(u_ref.dtype)


def _residual_prenorm_kernel(*refs, n_h):
    y_ref, gp_ref, g_ref, hn_ref, u_ref = refs[n_h:]
    hn = _stream_rows(refs[:n_h]) + _rms(y_ref[...], gp_ref[...], NORM_EPS)
    hn_ref[...] = hn
    u_ref[...] = _rms(hn, g_ref[...], NORM_EPS).astype(u_ref.dtype)


def _residual_shiftmix_kernel(h_ref, y_ref, hp_ref, yp_ref, gp_ref, g_ref, mu_ref, hn_ref, xs_ref):
    hn = h_ref[...] + _rms(y_ref[...], gp_ref[...], NORM_EPS)
    hn_ref[...] = hn
    u = _rms(hn, g_ref[...], NORM_EPS)
    last = _rms(hp_ref[7:8, :] + _rms(yp_ref[7:8, :], gp_ref[...], NORM_EPS), g_ref[...], NORM_EPS)
    last = jnp.where(pl.program_id(0) == 0, jnp.zeros_like(last), last)
    prev = jnp.where(_iota2(u.shape, 0) == 0, last, pltpu.roll(u, 1, 0))
    delta = prev - u
    for s in range(xs_ref.shape[0]):
        xs_ref[s] = (u + delta * mu_ref[s:s + 1, :]).astype(xs_ref.dtype)


def _residual_kernel(h_ref, y_ref, gp_ref, o_ref):
    o_ref[...] = h_ref[...] + _rms(y_ref[...], gp_ref[...], NORM_EPS)


def _prenorm(h, g, u_dtype):
    d = g.shape[0]
    lp = h[0].shape[0] if len(h) == 1 else ROW0 + h[1].shape[0]
    row = pl.BlockSpec((ROW_TILE, d), lambda i: (i, 0))
    vec = pl.BlockSpec((1, d), lambda i: (0, 0))
    return pl.pallas_call(
        functools.partial(_prenorm_kernel, n_h=len(h)), grid=(lp // ROW_TILE,),
        in_specs=_stream_specs(h) + [vec], out_specs=row,
        out_shape=jax.ShapeDtypeStruct((lp, d), u_dtype), compiler_params=_cparams(1), name="prenorm",
    )(*h, g.reshape(1, d))


def _residual_prenorm(h, y, gp, g, u_dtype):
    lp, d = y.shape
    row = pl.BlockSpec((ROW_TILE, d), lambda i: (i, 0))
    vec = pl.BlockSpec((1, d), lambda i: (0, 0))
    return pl.pallas_call(
        functools.partial(_residual_prenorm_kernel, n_h=len(h)), grid=(lp // ROW_TILE,),
        in_specs=_stream_specs(h) + [row, vec, vec], out_specs=[row, row],
        out_shape=[jax.ShapeDtypeStruct((lp, d), F32), jax.ShapeDtypeStruct((lp, d), u_dtype)],
        compiler_params=_cparams(1), name="residual_prenorm",
    )(*h, y, gp.reshape(1, d), g.reshape(1, d))


def _residual_shiftmix(h, y, gp, g, mu):
    lp, d = y.shape
    n = mu.shape[0]
    tile = ROW_TILE // 2
    per8 = tile // 8
    row = pl.BlockSpec((tile, d), lambda i: (i, 0))
    before = pl.BlockSpec((8, d), lambda i: (jnp.maximum(i * per8 - 1, 0), 0))
    vec = pl.BlockSpec((1, d), lambda i: (0, 0))
    return pl.pallas_call(
        _residual_shiftmix_kernel, grid=(lp // tile,),
        in_specs=[row, row, before, before, vec, vec, pl.BlockSpec((n, d), lambda i: (0, 0))],
        out_specs=[row, pl.BlockSpec((n, tile, d), lambda i: (0, i, 0))],
        out_shape=[jax.ShapeDtypeStruct((lp, d), F32), jax.ShapeDtypeStruct((n, lp, d), BF16)],
        compiler_params=_cparams(1), name="residual_shiftmix",
    )(h, y, h, y, gp.reshape(1, d), g.reshape(1, d), mu)


def _residual_out(h, y, gp, seq):
    lp, d = h.shape
    off = ROW0 // ROW_TILE
    row_in = pl.BlockSpec((ROW_TILE, d), lambda i: (i + off, 0))
    vec = pl.BlockSpec((1, d), lambda i: (0, 0))
    return pl.pallas_call(
        _residual_kernel, grid=(seq // ROW_TILE,), in_specs=[row_in, row_in, vec],
        out_specs=pl.BlockSpec((ROW_TILE, d), lambda i: (i, 0)),
        out_shape=jax.ShapeDtypeStruct((seq, d), F32), compiler_params=_cparams(1), name="residual_out",
    )(h, y, gp.reshape(1, d))


def _mm_kernel(a_ref, w_ref, o_ref):
    o_ref[...] = jnp.dot(a_ref[...], w_ref[...].astype(BF16), preferred_element_type=F32).astype(o_ref.dtype)


def _row_block(m, cap=1536):
    n = m // ROW_TILE
    best = max(d for d in range(1, n + 1) if n % d == 0 and ROW_TILE * d <= cap)
    return ROW_TILE * best


def _lhs_spec(a, tm, lead, index_map):
    k = a.shape[-1]
    if a.ndim == 2:
        return pl.BlockSpec((tm, k), lambda *g: (index_map(*g), 0))
    return pl.BlockSpec((None, tm, k), lambda *g: (lead, index_map(*g), 0))


def _matmul(a, w, layer, n_out, col_off=0, lead=0, out_dtype=F32, tn=512):
    m = a.shape[-2]
    k = a.shape[-1]
    tm = _row_block(m)
    tn = min(tn, n_out)
    joff = col_off // tn
    return pl.pallas_call(
        _mm_kernel, grid=(m // tm, n_out // tn),
        in_specs=[_lhs_spec(a, tm, lead, lambda i, j: i),
                  pl.BlockSpec((None, k, tn), lambda i, j: (layer, 0, j + joff))],
        out_specs=pl.BlockSpec((tm, tn), lambda i, j: (i, j)),
        out_shape=jax.ShapeDtypeStruct((m, n_out), out_dtype), compiler_params=_cparams(2), name="matmul",
    )(a, w)


def _rope(x, cos, sin):
    first = _iota2(x.shape, 1) < ROPE_DIM // 2
    other = jnp.where(first, pltpu.roll(x, B_QK_DIM - ROPE_DIM // 2, 1), pltpu.roll(x, ROPE_DIM // 2, 1))
    return x * cos + other * sin


def _rope_kernel(p_ref, cos_ref, sin_ref, o_ref):
    fac = jnp.where(pl.program_id(1) == 0, Q_SCALE_LOG2, 1.0)
    cos = cos_ref[...] * fac
    sin = sin_ref[...] * fac
    for c in range(p_ref.shape[1] // B_QK_DIM):
        cols = slice(c * B_QK_DIM, (c + 1) * B_QK_DIM)
        o_ref[:, cols] = _rope(p_ref[:, cols].astype(F32), cos, sin).astype(o_ref.dtype)


def _rope_qk(p, cos_t, sin_t, d):
    lp = p.shape[0]
    table = pl.BlockSpec((ROW_TILE, B_QK_DIM), lambda r, c: (r, 0))
    return pl.pallas_call(
        _rope_kernel, grid=(lp // ROW_TILE, 2),
        in_specs=[pl.BlockSpec((ROW_TILE, d), lambda r, c: (r, c)), table, table],
        out_specs=pl.BlockSpec((ROW_TILE, d), lambda r, c: (r, c)),
        out_shape=jax.ShapeDtypeStruct((lp, 2 * d), BF16), compiler_params=_cparams(2), name="rope_qk",
    )(p, cos_t, sin_t)


def _lora_kernel(x_ref, a_ref, b_ref, o_ref, *, use_tanh):
    mid = jnp.dot(x_ref[...], a_ref[...], preferred_element_type=F32)
    if use_tanh:
        mid = jnp.tanh(mid)
    o_ref[...] = _dot(mid, b_ref[...])


def _lora(x, lead, a, b, use_tanh):
    m, d = x.shape[-2:]
    r = a.shape[1]
    return pl.pallas_call(
        functools.partial(_lora_kernel, use_tanh=use_tanh), grid=(m // ROW_TILE,),
        in_specs=[_lhs_spec(x, ROW_TILE, lead, lambda i: i), pl.BlockSpec((d, r), lambda i: (0, 0)),
                  pl.BlockSpec((r, d), lambda i: (0, 0))],
        out_specs=pl.BlockSpec((ROW_TILE, d), lambda i: (i, 0)),
        out_shape=jax.ShapeDtypeStruct((m, d), F32), compiler_params=_cparams(1), name="lora",
    )(x, a, b)


HGRN2_LEVELS = tuple(CHUNK >> (i + 1) for i in range(int(math.log2(CHUNK))))


def _hgrn2_range_sums():
    t = np.arange(CHUNK)[:, None]
    c = np.arange(CHUNK)[None, :]
    blocks = [c <= t]
    for h in HGRN2_LEVELS:
        r = (t // (2 * h)) * 2 * h + h
        blocks.append(np.where(t >= r, (c > r) & (c <= t), (c > t) & (c <= r)))
    return jnp.asarray(np.concatenate(blocks, axis=0), dtype=BF16)


def _hgrn2_kernel(q_ref, f_ref, i_ref, g_ref, lb_ref, gn_ref, rs_ref, o_ref, st_ref, *, heads, chunks):
    @pl.when(pl.program_id(1) == 0)
    def _():
        st_ref[...] = jnp.zeros_like(st_ref)

    dh = A_HEAD_DIM
    width = heads * dh
    n_lev = len(HGRN2_LEVELS)
    ti = _iota2((CHUNK, CHUNK), 0)
    si = _iota2((CHUNK, CHUNK), 1)
    eye = ti == si
    pair_masks = []
    for h in HGRN2_LEVELS:
        sh = int(math.log2(h))
        pair_masks.append((((ti >> sh) & 1) == 1) & ((ti >> (sh + 1)) == (si >> (sh + 1))) & (((si >> sh) & 1) == 0))
    trow = _iota2((CHUNK, width), 0)
    upper = [((trow >> int(math.log2(h))) & 1) == 1 for h in HGRN2_LEVELS]
    lb = lb_ref[...]
    head_cols = [slice(h * dh, (h + 1) * dh) for h in range(heads)]

    tok = []
    for c in range(chunks):
        rows = slice(c * CHUNK, (c + 1) * CHUNK)
        q = q_ref[rows, :]
        v = i_ref[rows, :]
        f = lb + (1.0 - lb) * jax.nn.sigmoid(f_ref[rows, :])
        g = jnp.log(f)
        k = 1.0 - f
        g_hi = g.astype(BF16)
        g_lo = (g - g_hi.astype(F32)).astype(BF16)
        e2 = jnp.dot(rs_ref[...], jnp.concatenate([g_hi, g_lo], axis=1), preferred_element_type=F32)
        e = e2[:, :width] + e2[:, width:]
        b = e[0:CHUNK]
        b_last = b[CHUNK - 1:CHUNK, :]
        ex = jnp.exp(e)
        xs = [jnp.where(upper[i], q, k) * ex[(i + 1) * CHUNK:(i + 2) * CHUNK] for i in range(n_lev)]
        tok.append(dict(q=q, k=k, v=v, qe=q * ex[0:CHUNK], kd=k * jnp.exp(b_last - b), decay=jnp.exp(b_last), xs=xs))

    pairs = [(c, cs) for c in range(chunks) for cs in head_cols]
    scores = [jnp.where(eye, _dot_nt(tok[c]["q"][:, cs], tok[c]["k"][:, cs]), 0.0) for c, cs in pairs]
    for i in range(n_lev):
        scores = [sc + jnp.where(pair_masks[i], _dot_nt(tok[c]["xs"][i][:, cs], tok[c]["xs"][i][:, cs]), 0.0)
                  for sc, (c, cs) in zip(scores, pairs)]
    intra = [_dot(sc, tok[c]["v"][:, cs]) for sc, (c, cs) in zip(scores, pairs)]
    kv = [_dot_tn(tok[c]["v"][:, cs], tok[c]["kd"][:, cs]) for c, cs in pairs]
    for n, (c, cs) in enumerate(pairs):
        h = n % heads
        rows = slice(c * CHUNK, (c + 1) * CHUNK)
        st = st_ref[h]
        o = _dot_nt(tok[c]["qe"][:, cs], st) + intra[n]
        st_ref[h] = st * tok[c]["decay"][:, cs] + kv[n]
        o = _rms(o, gn_ref[...], NORM_EPS)
        o_ref[rows, cs] = (o * _silu(g_ref[rows, cs])).astype(o_ref.dtype)


def _hgrn2_core(p, lb, gnorm_g, d):
    lp = p.shape[0]
    heads = 4
    width = heads * A_HEAD_DIM
    ncol = d // width
    blk = lambda part: pl.BlockSpec((ROW_TILE, width), lambda j, r: (r, part * ncol + j))
    range_sums = _hgrn2_range_sums()
    return pl.pallas_call(
        functools.partial(_hgrn2_kernel, heads=heads, chunks=ROW_TILE // CHUNK),
        grid=(ncol, lp // ROW_TILE),
        in_specs=[blk(0), blk(1), blk(2), blk(3), pl.BlockSpec((1, width), lambda j, r: (0, j)),
                  pl.BlockSpec((1, A_HEAD_DIM), lambda j, r: (0, 0)),
                  pl.BlockSpec(range_sums.shape, lambda j, r: (0, 0))],
        out_specs=pl.BlockSpec((ROW_TILE, width), lambda j, r: (r, j)),
        out_shape=jax.ShapeDtypeStruct((lp, d), BF16),
        scratch_shapes=[pltpu.VMEM((heads, A_HEAD_DIM, A_HEAD_DIM), F32)],
        compiler_params=_cparams(2), name="hgrn2",
    )(p, p, p, p, lb.reshape(1, d), gnorm_g.reshape(1, A_HEAD_DIM), range_sums)


def _row_chunk_id(row):
    return jnp.maximum((row - (ROW0 - CHUNK)) >> int(math.log2(CHUNK)), 0)


def _attn_kernel(q_ref, k_ref, v_ref, g_ref, lam_ref, sg_ref, o_ref, qx_ref, s_ref, acc_ref, m_ref, l_ref, *,
                 tq, lambda_init):
    i = pl.program_id(1)
    q = q_ref[...]
    lane = _iota2(q.shape, 1)
    qx_ref[0:tq, :] = jnp.where(lane < B_QK_DIM, q, jnp.zeros_like(q))
    qx_ref[tq:2 * tq, :] = jnp.where(lane >= B_QK_DIM, q, jnp.zeros_like(q))
    acc_ref[...] = jnp.zeros_like(acc_ref)
    m_ref[...] = jnp.full_like(m_ref, MASK_VALUE)
    l_ref[...] = jnp.zeros_like(l_ref)

    def lane_fold(x, op):
        out = x[:, 0:128]
        for t in range(1, x.shape[1] // 128):
            out = op(out, x[:, t * 128:(t + 1) * 128])
        return out

    def lane_tile(x, n):
        return jnp.concatenate([x] * n, axis=1)

    def key_rows(row0, width):
        return pl.ds(row0 if isinstance(row0, int) else pl.multiple_of(row0, tq), width)

    def scores(row0, width):
        rows = key_rows(row0, width)
        return lax.dot_general(qx_ref[...], k_ref[rows, :], (((1,), (1,)), ((), ())), preferred_element_type=F32)

    q_chunk = _row_chunk_id(i * tq + (_iota2((2 * tq, 1), 0) & (tq - 1)))

    def mask_scores(s, row0):
        k_row = row0 + _iota2((1, s.shape[1]), 1)
        k_chunk = jnp.where(k_row >= ROW0 - N_META, _row_chunk_id(k_row), NO_CHUNK)
        return jnp.where(k_chunk <= q_chunk, s, MASK_VALUE)

    def soften(s):
        m_prev = m_ref[...]
        m_new = jnp.maximum(m_prev, jnp.max(lane_fold(s, jnp.maximum), axis=1, keepdims=True))
        alpha = jnp.exp2(m_prev - m_new)
        p = jnp.exp2(s - lane_tile(m_new, s.shape[1] // 128))
        l_ref[...] = alpha * l_ref[...] + lane_fold(p, jnp.add)
        m_ref[...] = m_new
        return p.astype(BF16), alpha

    def value_product(p, row0, width):
        return jnp.dot(p, v_ref[key_rows(row0, width), :], preferred_element_type=F32)

    def rescale_add(alpha, pv):
        acc_ref[...] = acc_ref[...] * lane_tile(alpha, B_V_DIM // 128) + pv

    first_key = (ROW0 - N_META) // 128 * 128
    head_width = tq - first_key

    def closing_step(s_last):
        row0 = jnp.maximum(i, 1) * tq
        p, alpha = soften(jnp.concatenate(
            [mask_scores(scores(first_key, head_width), first_key), mask_scores(s_last, row0)], axis=1))
        rescale_add(alpha, value_product(p[:, :head_width], first_key, head_width)
                    + value_product(p[:, head_width:], row0, tq))

    n_full = jnp.maximum(i - 1, 0)

    def tile_start(t):
        return tq * (t + 1)

    def unrolled_tiles(u, carry):
        for t in range(TILE_UNROLL):
            cur = t % 2
            s_ref[1 - cur] = scores(tile_start(TILE_UNROLL * u + t + 1), tq)
            p, alpha = soften(s_ref[cur])
            rescale_add(alpha, value_product(p, tq * (TILE_UNROLL * u + t + 1), tq))
        return carry

    def single_tile(t, carry):
        s_next = scores(tile_start(t + 1), tq)
        p, alpha = soften(s_ref[0])
        rescale_add(alpha, value_product(p, tq * (t + 1), tq))
        s_ref[0] = s_next
        return carry

    n_main = n_full // TILE_UNROLL
    s_ref[0] = scores(tile_start(0), tq)
    lax.fori_loop(0, n_main, unrolled_tiles, 0)
    lax.fori_loop(n_main * TILE_UNROLL, n_full, single_tile, 0)

    closing_step(s_ref[0])

    lam4 = lam_ref[...]
    lam = (jnp.exp(jnp.sum(lam4[0:1] * lam4[1:2], axis=1, keepdims=True))
           - jnp.exp(jnp.sum(lam4[2:3] * lam4[3:4], axis=1, keepdims=True)) + lambda_init)
    acc = acc_ref[...] / jnp.sum(l_ref[...], axis=1, keepdims=True)
    o = acc[0:tq] - lam * acc[tq:2 * tq]
    o = _rms(o, sg_ref[...], SUBLN_EPS) * (1.0 - lambda_init)
    o_ref[...] = (o * _silu(g_ref[...].astype(F32))).astype(o_ref.dtype)


def _attn_core(qk, p, lam4, subln_g, d, lambda_init):
    lp = p.shape[0]
    heads = d // B_V_DIM
    tq = ATTN_TILE
    assert ROW0 == tq and lp % tq == 0 and lp >= 2 * tq
    kernel = functools.partial(_attn_kernel, tq=tq, lambda_init=lambda_init)
    return pl.pallas_call(
        kernel, grid=(heads, lp // tq),
        in_specs=[pl.BlockSpec((tq, B_V_DIM), lambda h, i: (i, h)),
                  pl.BlockSpec((lp, B_V_DIM), lambda h, i: (0, heads + h)),
                  pl.BlockSpec((lp, B_V_DIM), lambda h, i: (0, 2 * heads + h)),
                  pl.BlockSpec((tq, B_V_DIM), lambda h, i: (i, 3 * heads + h)),
                  pl.BlockSpec((4, B_QK_DIM), lambda h, i: (0, 0)),
                  pl.BlockSpec((1, B_V_DIM), lambda h, i: (0, 0))],
        out_specs=pl.BlockSpec((tq, B_V_DIM), lambda h, i: (i, h)),
        out_shape=jax.ShapeDtypeStruct((lp, d), BF16),
        scratch_shapes=[pltpu.VMEM((2 * tq, B_V_DIM), BF16), pltpu.VMEM((2, 2 * tq, tq), F32),
                        pltpu.VMEM((2 * tq, B_V_DIM), F32),
                        pltpu.VMEM((2 * tq, 128), F32), pltpu.VMEM((2 * tq, 128), F32)],
        compiler_params=_cparams(2), name="diff_attn",
    )(qk, qk, p, p, lam4, subln_g.reshape(1, B_V_DIM))


def _rope_tables(lp):
    pos = (jnp.arange(lp) - (ROW0 - N_META)).astype(F32)
    inv_freq = ROPE_THETA ** (-jnp.arange(0, ROPE_DIM, 2, dtype=F32) / ROPE_DIM)
    ang = pos[:, None] * inv_freq[None, :]
    cos, sin = jnp.cos(ang), jnp.sin(ang)
    pad = B_QK_DIM - ROPE_DIM
    cos_t = jnp.concatenate([cos, cos, jnp.ones((lp, pad), F32)], axis=1)
    sin_t = jnp.concatenate([-sin, sin, jnp.zeros((lp, pad), F32)], axis=1)
    return cos_t, sin_t


RWKV_GROUP = 256 // C_HEAD_DIM
RWKV_GROUP_W = RWKV_GROUP * C_HEAD_DIM
RWKV_SKEW = 1


def _rwkv_kernel(r_ref, k_ref, v_ref, g_ref, dw_ref, da_ref, w0_ref, a0_ref, kk_ref, ka_ref, rk_ref,
                 lw_ref, lb_ref, o_ref, st_ref, *, groups, chunks):
    @pl.when(pl.program_id(1) == 0)
    def _():
        st_ref[...] = jnp.zeros_like(st_ref)

    n = C_HEAD_DIM
    gw = RWKV_GROUP_W
    width = groups * gw
    shift = int(math.log2(n))
    ri = _iota2((gw, gw), 0)
    ci = _iota2((gw, gw), 1)
    same = (ri >> shift) == (ci >> shift)
    seg = same.astype(BF16)

    def seg_sum(x):
        return jnp.concatenate([jnp.dot(x[:, g * gw:(g + 1) * gw].astype(BF16), seg, preferred_element_type=F32)
                                for g in range(groups)], axis=1)

    strict = same & ((ci & (n - 1)) < (ri & (n - 1)))
    eye = ri == ci
    wide_t = _iota2((CHUNK, gw), 0)
    wide_s = _iota2((CHUNK, gw), 1) & (n - 1)
    strict_w = wide_s < wide_t
    lower_w = wide_s <= wide_t

    def tile_rows(y):
        return jnp.concatenate([y] * RWKV_GROUP, axis=0)

    def expand(y):
        return jnp.where(same, tile_rows(y), 0.0)

    def compact(e):
        out = e[0:CHUNK]
        for h in range(1, RWKV_GROUP):
            out = out + e[h * CHUNK:(h + 1) * CHUNK]
        return out

    tok = {}
    per = {}

    def part(c, g, name):
        return tok[c][name][:, g * gw:(g + 1) * gw]

    def prepare(c):
        rows = slice(c * CHUNK, (c + 1) * CHUNK)
        r = r_ref[rows, :]
        k = k_ref[rows, :]
        v = v_ref[rows, :]
        logw = -C_DECAY_SCALE * jax.nn.sigmoid(w0_ref[...] + dw_ref[rows, :])
        alpha = jax.nn.sigmoid(a0_ref[...] + da_ref[rows, :])
        kk = k * kk_ref[...]
        kk = kk / jnp.maximum(jnp.sqrt(seg_sum(kk * kk)), 1e-12)
        k2 = k * (1.0 + (alpha - 1.0) * ka_ref[...])
        cum = _chunk_cumsum(logw)
        cum_last = cum[CHUNK - 1:CHUNK, :]
        b = kk * alpha
        e_inv = jnp.exp(-cum)
        e_tail = jnp.exp(cum_last - cum)
        tok[c] = dict(r=r, v=v, k2=k2, p_t=-kk * jnp.exp(cum - logw), r_t=r * jnp.exp(cum), b_t=b * e_inv,
                      k_t=k2 * e_inv, b_h=b * e_tail, k_h=k2 * e_tail, gam=jnp.exp(cum_last))
        for g in range(groups):
            x = per[c, g] = dict(ep=expand(part(c, g, "p_t")), ev=expand(part(c, g, "v")))
            x["a_all"] = _dot_nt(jnp.concatenate([part(c, g, "p_t"), part(c, g, "r_t")], axis=0),
                                 jnp.concatenate([expand(part(c, g, "b_t")), expand(part(c, g, "k_t"))], axis=0))

    def split(c):
        for g in range(groups):
            x = per[c, g]
            a_all = x.pop("a_all")
            x["power"] = jnp.where(strict, tile_rows(a_all[:CHUNK, :gw]), 0.0)
            x["t_inv"] = jnp.where(eye, 1.0, x["power"])
            x["a3"] = jnp.where(lower_w, a_all[CHUNK:, :gw], 0.0)
            x["l2v"] = expand(_dot(jnp.where(strict_w, a_all[:CHUNK, gw:], 0.0), x["ev"]))
            x["a4v"] = _dot(jnp.where(lower_w, a_all[CHUNK:, gw:], 0.0), x["ev"])

    def neumann(c):
        for g in range(groups):
            x = per[c, g]
            x["power"] = _dot(x["power"], x["power"])
            x["t_inv"] = x["t_inv"] + _dot(x["t_inv"], x["power"])

    def solve(c):
        for g in range(groups):
            x = per[c, g]
            x["wv"] = _dot(x["t_inv"], jnp.concatenate([x["ep"], x["l2v"]], axis=1))

    def finish(c):
        for g in range(groups):
            x = per[c, g]
            qo = _dot(x["a3"], x["wv"])
            x["q_t"] = part(c, g, "r_t") + qo[:, :gw]
            x["o_intra"] = qo[:, gw:] + x["a4v"]
            x["w_c"] = compact(x["wv"][:, :gw])
            x["v_c"] = compact(x["wv"][:, gw:])
            x["bk"] = jnp.concatenate([part(c, g, "b_h"), part(c, g, "k_h")], axis=0)
            x["gam_col"] = jnp.sum(jnp.where(eye, part(c, g, "gam"), 0.0), axis=1, keepdims=True)

    def read_state(c):
        for g in range(groups):
            x = per[c, g]
            st = st_ref[g]
            x["o"] = _dot(x["q_t"], st) + x["o_intra"]
            x["u_c"] = _dot(x["w_c"], st) + x["v_c"]
            x["st_decayed"] = st * x["gam_col"]

    def write_state(c):
        rows = slice(c * CHUNK, (c + 1) * CHUNK)
        for g in range(groups):
            x = per[c, g]
            update = _dot_tn(x["bk"], jnp.concatenate([x["u_c"], part(c, g, "v")], axis=0))
            st_ref[g] = x["st_decayed"] + jnp.where(same, update, 0.0)
        o = jnp.concatenate([per[c, g]["o"] for g in range(groups)], axis=1)
        mean = seg_sum(o) * (1.0 / n)
        cen = o - mean
        var = seg_sum(cen * cen) * (1.0 / n)
        o = cen * lax.rsqrt(var + LNX_EPS) * lw_ref[...] + lb_ref[...]
        o = o + seg_sum(tok[c]["r"] * tok[c]["k2"] * rk_ref[...]) * tok[c]["v"]
        o_ref[rows, :] = (o * _silu(g_ref[rows, :])).astype(o_ref.dtype)

    stages = [prepare, split] + [neumann] * (int(math.log2(CHUNK)) - 1) + [solve, finish, read_state, write_state]
    for tick in range(len(stages) + RWKV_SKEW * (chunks - 1)):
        for c in range(chunks):
            s = tick - RWKV_SKEW * c
            if 0 <= s < len(stages):
                stages[s](c)


def _rwkv_core(r, k, v, gate, dw, da, vecs, d):
    lp = r.shape[0]
    groups = min(4, d // RWKV_GROUP_W)
    chunks = 4
    width = groups * RWKV_GROUP_W
    rows = chunks * CHUNK
    act = pl.BlockSpec((rows, width), lambda j, c: (c, j))
    vec = pl.BlockSpec((1, width), lambda j, c: (0, j))
    return pl.pallas_call(
        functools.partial(_rwkv_kernel, groups=groups, chunks=chunks), grid=(d // width, lp // rows),
        in_specs=[act] * 6 + [vec] * 7, out_specs=act,
        out_shape=jax.ShapeDtypeStruct((lp, d), BF16),
        scratch_shapes=[pltpu.VMEM((groups, RWKV_GROUP_W, RWKV_GROUP_W), F32)],
        compiler_params=_cparams(2), name="rwkv7",
    )(r, k, v, gate, dw, da, *[t.reshape(1, d) for t in vecs])


def _hgrn2_layer(u, w_in, j, lb, gnorm_g, w_out, d):
    p = _matmul(u, w_in, j, 4 * d)
    return _matmul(_hgrn2_core(p, lb, gnorm_g, d), w_out, j, d)


def _attn_layer(u, w_in, j, lam4, subln_g, w_out, d, lambda_init, tables):
    p = _matmul(u, w_in, j, 4 * d, out_dtype=BF16)
    qk = _rope_qk(p, *tables, d)
    return _matmul(_attn_core(qk, p, lam4, subln_g, d, lambda_init), w_out, j, d)


def _rwkv_layer(xs, w_in, j, w0, w1, w2, a0, a1, a2, k_k, k_a, r_k, lnx_w, lnx_b, w_out, d):
    r, k, v, gate = (_matmul(xs, w_in, j, d, col_off=s * d, lead=s) for s in range(4))
    dw = _lora(xs, 4, w1.astype(BF16), w2.astype(BF16), True)
    da = _lora(xs, 5, a1.astype(BF16), a2.astype(BF16), False)
    o = _rwkv_core(r, k, v, gate, dw, da, (w0, a0, k_k, k_a, r_k.reshape(-1), lnx_w, lnx_b), d)
    return _matmul(o, w_out, j, d)


def kernel(x, meta_tokens, pre_norm_g, post_norm_g, a_w_in, a_lb_logits, a_gnorm_g, a_w_out, b_w_in, b_lam_q1, b_lam_k1, b_lam_q2, b_lam_k2, b_subln_g, b_w_out, c_mu, c_w_in, c_w0, c_w1, c_w2, c_a0, c_a1, c_a2, c_k_k, c_k_a, c_r_k, c_lnx_w, c_lnx_b, c_w_out):
    bsz, seq, d = x.shape
    depth = pre_norm_g.shape[0]
    assert bsz == 1 and seq % ATTN_TILE == 0 and meta_tokens.shape[0] == N_META and depth >= 2
    lp = ROW0 + seq
    h = (jnp.concatenate([jnp.zeros((ROW0 - N_META, d), F32), meta_tokens.astype(F32)], axis=0), x[0])
    tables = _rope_tables(lp)
    lb_p = jax.nn.softmax(a_lb_logits.astype(F32), axis=0)
    lb_all = jnp.cumsum(lb_p, axis=0) - lb_p[0]

    u = _prenorm(h, pre_norm_g[0], BF16)
    for i in range(depth):
        kind, j = i % N_MIXERS, i // N_MIXERS
        if kind == 0:
            y = _hgrn2_layer(u, a_w_in, j, lb_all[j], a_gnorm_g[j], a_w_out, d)
        elif kind == 1:
            lambda_init = 0.8 - 0.6 * math.exp(-0.3 * i)
            lam4 = jnp.stack([b_lam_q1[j], b_lam_k1[j], b_lam_q2[j], b_lam_k2[j]]).astype(F32)
            y = _attn_layer(u, b_w_in, j, lam4, b_subln_g[j], b_w_out, d, lambda_init, tables)
        else:
            y = _rwkv_layer(u, c_w_in, j, c_w0[j], c_w1[j], c_w2[j], c_a0[j], c_a1[j], c_a2[j],
                            c_k_k[j], c_k_a[j], c_r_k[j], c_lnx_w[j], c_lnx_b[j], c_w_out, d)
        if i + 1 == depth:
            out = _residual_out(h[0], y, post_norm_g[i], seq)
        elif (i + 1) % N_MIXERS == 2:
            hn, u = _residual_shiftmix(h[0], y, post_norm_g[i], pre_norm_g[i + 1], c_mu[(i + 1) // N_MIXERS])
            h = (hn,)
        else:
            hn, u = _residual_prenorm(h, y, post_norm_g[i], pre_norm_g[i + 1], BF16)
            h = (hn,)
    return out[None]
```

```python
import functools
import math

import jax
import jax.numpy as jnp
import numpy as np
from jax import lax
from jax.experimental import pallas as pl
from jax.experimental.pallas import tpu as pltpu

F32 = jnp.float32
BF16 = jnp.bfloat16

CHUNK = 64
N_META = 16
N_MIXERS = 3
NORM_EPS = 1e-6
A_HEAD_DIM = 128
B_QK_DIM = 128
B_V_DIM = 2 * B_QK_DIM
ROPE_THETA = 500000.0
ROPE_DIM = B_QK_DIM // 4
SUBLN_EPS = 1e-5
C_HEAD_DIM = 64
C_DECAY_SCALE = 0.606531
LNX_EPS = 64e-5

ROW0 = 512
ROW_TILE = 256
ATTN_TILE = 512
MASK_VALUE = -1e30
NO_CHUNK = 1 << 30
TILE_UNROLL = 4
Q_SCALE_LOG2 = B_QK_DIM ** -0.5 * math.log2(math.e)
V7X_VMEM_LIMIT = 56 * 1024 * 1024


def _cparams(n_axes, vmem=V7X_VMEM_LIMIT):
    return pltpu.CompilerParams(dimension_semantics=("arbitrary",) * n_axes, vmem_limit_bytes=vmem)


def _dot(a, b):
    return jnp.dot(a.astype(BF16), b.astype(BF16), preferred_element_type=F32)


def _dot_nt(a, b):
    return lax.dot_general(a.astype(BF16), b.astype(BF16), (((1,), (1,)), ((), ())), preferred_element_type=F32)


def _dot_tn(a, b):
    return lax.dot_general(a.astype(BF16), b.astype(BF16), (((0,), (0,)), ((), ())), preferred_element_type=F32)


def _iota2(shape, axis):
    return lax.broadcasted_iota(jnp.int32, shape, axis)


def _chunk_cumsum(x):
    tri = (_iota2((CHUNK, CHUNK), 1) <= _iota2((CHUNK, CHUNK), 0)).astype(BF16)
    hi = x.astype(BF16)
    lo = (x - hi.astype(F32)).astype(BF16)
    return jnp.dot(tri, hi, preferred_element_type=F32) + jnp.dot(tri, lo, preferred_element_type=F32)


def _silu(x):
    return x * jax.nn.sigmoid(x)


def _rms(x, g, eps):
    return x * lax.rsqrt(jnp.mean(x * x, axis=-1, keepdims=True) + eps) * g


HEAD_TILES = ROW0 // ROW_TILE


def _stream_rows(refs):
    if len(refs) == 1:
        return refs[0][...]
    head_ref, frames_ref = refs
    return jnp.where(pl.program_id(0) < HEAD_TILES, head_ref[...], frames_ref[...])


def _stream_specs(h):
    d = h[0].shape[1]
    if len(h) == 1:
        return [pl.BlockSpec((ROW_TILE, d), lambda i: (i, 0))]
    return [pl.BlockSpec((ROW_TILE, d), lambda i: (jnp.minimum(i, HEAD_TILES - 1), 0)),
            pl.BlockSpec((ROW_TILE, d), lambda i: (jnp.maximum(i - HEAD_TILES, 0), 0))]


def _prenorm_kernel(*refs, n_h):
    g_ref, u_ref = refs[n_h:]
    u_ref[...] = _rms(_stream_rows(refs[:n_h]), g_ref[...], NORM_EPS).astype(u_ref.dtype)


def _residual_prenorm_kernel(*refs, n_h):
    y_ref, gp_ref, g_ref, hn_ref, u_ref = refs[n_h:]
    hn = _stream_rows(refs[:n_h]) + _rms(y_ref[...].astype(F32), gp_ref[...], NORM_EPS)
    hn_ref[...] = hn
    u_ref[...] = _rms(hn, g_ref[...], NORM_EPS).astype(u_ref.dtype)


def _residual_shiftmix_kernel(h_ref, y_ref, hp_ref, yp_ref, gp_ref, g_ref, mu_ref, hn_ref, xs_ref):
    hn = h_ref[...] + _rms(y_ref[...].astype(F32), gp_ref[...], NORM_EPS)
    hn_ref[...] = hn
    u = _rms(hn, g_ref[...], NORM_EPS)
    last = _rms(hp_ref[7:8, :] + _rms(yp_ref[7:8, :], gp_ref[...], NORM_EPS), g_ref[...], NORM_EPS)
    last = jnp.where(pl.program_id(0) == 0, jnp.zeros_like(last), last)
    prev = jnp.where(_iota2(u.shape, 0) == 0, last, pltpu.roll(u, 1, 0))
    delta = prev - u
    for s in range(xs_ref.shape[0]):
        xs_ref[s] = (u + delta * mu_ref[s:s + 1, :]).astype(xs_ref.dtype)


def _residual_kernel(h_ref, y_ref, gp_ref, o_ref):
    o_ref[...] = h_ref[...] + _rms(y_ref[...].astype(F32), gp_ref[...], NORM_EPS)


def _prenorm(h, g, u_dtype):
    d = g.shape[0]
    lp = h[0].shape[0] if len(h) == 1 else ROW0 + h[1].shape[0]
    row = pl.BlockSpec((ROW_TILE, d), lambda i: (i, 0))
    vec = pl.BlockSpec((1, d), lambda i: (0, 0))
    return pl.pallas_call(
        functools.partial(_prenorm_kernel, n_h=len(h)), grid=(lp // ROW_TILE,),
        in_specs=_stream_specs(h) + [vec], out_specs=row,
        out_shape=jax.ShapeDtypeStruct((lp, d), u_dtype), compiler_params=_cparams(1), name="prenorm",
    )(*h, g.reshape(1, d))


def _residual_prenorm(h, y, gp, g, u_dtype):
    lp, d = y.shape
    row = pl.BlockSpec((ROW_TILE, d), lambda i: (i, 0))
    vec = pl.BlockSpec((1, d), lambda i: (0, 0))
    return pl.pallas_call(
        functools.partial(_residual_prenorm_kernel, n_h=len(h)), grid=(lp // ROW_TILE,),
        in_specs=_stream_specs(h) + [row, vec, vec], out_specs=[row, row],
        out_shape=[jax.ShapeDtypeStruct((lp, d), F32), jax.ShapeDtypeStruct((lp, d), u_dtype)],
        compiler_params=_cparams(1), name="residual_prenorm",
    )(*h, y, gp.reshape(1, d), g.reshape(1, d))


def _residual_shiftmix(h, y, gp, g, mu):
    lp, d = y.shape
    n = mu.shape[0]
    tile = ROW_TILE // 2
    per8 = tile // 8
    row = pl.BlockSpec((tile, d), lambda i: (i, 0))
    before = pl.BlockSpec((8, d), lambda i: (jnp.maximum(i * per8 - 1, 0), 0))
    vec = pl.BlockSpec((1, d), lambda i: (0, 0))
    return pl.pallas_call(
        _residual_shiftmix_kernel, grid=(lp // tile,),
        in_specs=[row, row, before, before, vec, vec, pl.BlockSpec((n, d), lambda i: (0, 0))],
        out_specs=[row, pl.BlockSpec((n, tile, d), lambda i: (0, i, 0))],
        out_shape=[jax.ShapeDtypeStruct((lp, d), F32), jax.ShapeDtypeStruct((n, lp, d), BF16)],
        compiler_params=_cparams(1), name="residual_shiftmix",
    )(h, y, h, y, gp.reshape(1, d), g.reshape(1, d), mu)


def _residual_out(h, y, gp, seq):
    lp, d = h.shape
    off = ROW0 // ROW_TILE
    row_in = pl.BlockSpec((ROW_TILE, d), lambda i: (i + off, 0))
    vec = pl.BlockSpec((1, d), lambda i: (0, 0))
    return pl.pallas_call(
        _residual_kernel, grid=(seq // ROW_TILE,), in_specs=[row_in, row_in, vec],
        out_specs=pl.BlockSpec((ROW_TILE, d), lambda i: (i, 0)),
        out_shape=jax.ShapeDtypeStruct((seq, d), F32), compiler_params=_cparams(1), name="residual_out",
    )(h, y, gp.reshape(1, d))


def _mm_kernel(a_ref, w_ref, o_ref):
    o_ref[...] = jnp.dot(a_ref[...], w_ref[...].astype(BF16), preferred_element_type=F32).astype(o_ref.dtype)


def _row_block(m, cap=1536):
    n = m // ROW_TILE
    best = max(d for d in range(1, n + 1) if n % d == 0 and ROW_TILE * d <= cap)
    return ROW_TILE * best


def _lhs_spec(a, tm, lead, index_map):
    k = a.shape[-1]
    if a.ndim == 2:
        return pl.BlockSpec((tm, k), lambda *g: (index_map(*g), 0))
    return pl.BlockSpec((None, tm, k), lambda *g: (lead, index_map(*g), 0))


def _matmul(a, w, layer, n_out, col_off=0, lead=0, out_dtype=F32, tn=512):
    m = a.shape[-2]
    k = a.shape[-1]
    tm = _row_block(m)
    tn = min(tn, n_out)
    joff = col_off // tn
    return pl.pallas_call(
        _mm_kernel, grid=(m // tm, n_out // tn),
        in_specs=[_lhs_spec(a, tm, lead, lambda i, j: i),
                  pl.BlockSpec((None, k, tn), lambda i, j: (layer, 0, j + joff))],
        out_specs=pl.BlockSpec((tm, tn), lambda i, j: (i, j)),
        out_shape=jax.ShapeDtypeStruct((m, n_out), out_dtype), compiler_params=_cparams(2), name="matmul",
    )(a, w)


def _rope(x, cos, sin):
    first = _iota2(x.shape, 1) < ROPE_DIM // 2
    other = jnp.where(first, pltpu.roll(x, B_QK_DIM - ROPE_DIM // 2, 1), pltpu.roll(x, ROPE_DIM // 2, 1))
    return x * cos + other * sin


def _rope_kernel(p_ref, cos_ref, sin_ref, o_ref):
    fac = jnp.where(pl.program_id(1) == 0, Q_SCALE_LOG2, 1.0)
    cos = cos_ref[...] * fac
    sin = sin_ref[...] * fac
    for c in range(p_ref.shape[1] // B_QK_DIM):
        cols = slice(c * B_QK_DIM, (c + 1) * B_QK_DIM)
        o_ref[:, cols] = _rope(p_ref[:, cols].astype(F32), cos, sin).astype(o_ref.dtype)


def _rope_qk(p, cos_t, sin_t, d):
    lp = p.shape[0]
    table = pl.BlockSpec((ROW_TILE, B_QK_DIM), lambda r, c: (r, 0))
    return pl.pallas_call(
        _rope_kernel, grid=(lp // ROW_TILE, 2),
        in_specs=[pl.BlockSpec((ROW_TILE, d), lambda r, c: (r, c)), table, table],
        out_specs=pl.BlockSpec((ROW_TILE, d), lambda r, c: (r, c)),
        out_shape=jax.ShapeDtypeStruct((lp, 2 * d), BF16), compiler_params=_cparams(2), name="rope_qk",
    )(p, cos_t, sin_t)


def _lora_kernel(x_ref, a_ref, b_ref, o_ref, *, use_tanh):
    mid = jnp.dot(x_ref[...], a_ref[...], preferred_element_type=F32)
    if use_tanh:
        mid = jnp.tanh(mid)
    o_ref[...] = _dot(mid, b_ref[...]).astype(o_ref.dtype)


def _lora(x, lead, a, b, use_tanh):
    m, d = x.shape[-2:]
    r = a.shape[1]
    return pl.pallas_call(
        functools.partial(_lora_kernel, use_tanh=use_tanh), grid=(m // ROW_TILE,),
        in_specs=[_lhs_spec(x, ROW_TILE, lead, lambda i: i), pl.BlockSpec((d, r), lambda i: (0, 0)),
                  pl.BlockSpec((r, d), lambda i: (0, 0))],
        out_specs=pl.BlockSpec((ROW_TILE, d), lambda i: (i, 0)),
        out_shape=jax.ShapeDtypeStruct((m, d), BF16), compiler_params=_cparams(1), name="lora",
    )(x, a, b)


HGRN2_LEVELS = tuple(CHUNK >> (i + 1) for i in range(int(math.log2(CHUNK))))


def _hgrn2_range_sums():
    t = np.arange(CHUNK)[:, None]
    c = np.arange(CHUNK)[None, :]
    blocks = [c <= t]
    for h in HGRN2_LEVELS:
        r = (t // (2 * h)) * 2 * h + h
        blocks.append(np.where(t >= r, (c > r) & (c <= t), (c > t) & (c <= r)))
    return jnp.asarray(np.concatenate(blocks, axis=0), dtype=BF16)


def _hgrn2_kernel(q_ref, f_ref, i_ref, g_ref, lb_ref, gn_ref, rs_ref, o_ref, st_ref, *, heads, chunks):
    @pl.when(pl.program_id(1) == 0)
    def _():
        st_ref[...] = jnp.zeros_like(st_ref)

    dh = A_HEAD_DIM
    width = heads * dh
    n_lev = len(HGRN2_LEVELS)
    ti = _iota2((CHUNK, CHUNK), 0)
    si = _iota2((CHUNK, CHUNK), 1)
    eye = ti == si
    pair_masks = []
    for h in HGRN2_LEVELS:
        sh = int(math.log2(h))
        pair_masks.append((((ti >> sh) & 1) == 1) & ((ti >> (sh + 1)) == (si >> (sh + 1))) & (((si >> sh) & 1) == 0))
    trow = _iota2((CHUNK, width), 0)
    upper = [((trow >> int(math.log2(h))) & 1) == 1 for h in HGRN2_LEVELS]
    lb = lb_ref[...]
    head_cols = [slice(h * dh, (h + 1) * dh) for h in range(heads)]

    tok = []
    for c in range(chunks):
        rows = slice(c * CHUNK, (c + 1) * CHUNK)
        q = q_ref[rows, :]
        v = i_ref[rows, :]
        f = lb + (1.0 - lb) * jax.nn.sigmoid(f_ref[rows, :])
        g = jnp.log(f)
        k = 1.0 - f
        g_hi = g.astype(BF16)
        g_lo = (g - g_hi.astype(F32)).astype(BF16)
        e2 = jnp.dot(rs_ref[...], jnp.concatenate([g_hi, g_lo], axis=1), preferred_element_type=F32)
        e = e2[:, :width] + e2[:, width:]
        b = e[0:CHUNK]
        b_last = b[CHUNK - 1:CHUNK, :]
        ex = jnp.exp(e)
        xs = [jnp.where(upper[i], q, k) * ex[(i + 1) * CHUNK:(i + 2) * CHUNK] for i in range(n_lev)]
        tok.append(dict(q=q, k=k, v=v, qe=q * ex[0:CHUNK], kd=k * jnp.exp(b_last - b), decay=jnp.exp(b_last), xs=xs))

    pairs = [(c, cs) for c in range(chunks) for cs in head_cols]
    scores = [jnp.where(eye, _dot_nt(tok[c]["q"][:, cs], tok[c]["k"][:, cs]), 0.0) for c, cs in pairs]
    for i in range(n_lev):
        scores = [sc + jnp.where(pair_masks[i], _dot_nt(tok[c]["xs"][i][:, cs], tok[c]["xs"][i][:, cs]), 0.0)
                  for sc, (c, cs) in zip(scores, pairs)]
    intra = [_dot(sc, tok[c]["v"][:, cs]) for sc, (c, cs) in zip(scores, pairs)]
    kv = [_dot_tn(tok[c]["v"][:, cs], tok[c]["kd"][:, cs]) for c, cs in pairs]
    for n, (c, cs) in enumerate(pairs):
        h = n % heads
        rows = slice(c * CHUNK, (c + 1) * CHUNK)
        st = st_ref[h]
        o = _dot_nt(tok[c]["qe"][:, cs], st) + intra[n]
        st_ref[h] = st * tok[c]["decay"][:, cs] + kv[n]
        o = _rms(o, gn_ref[...], NORM_EPS)
        o_ref[rows, cs] = (o * _silu(g_ref[rows, cs])).astype(o_ref.dtype)


def _hgrn2_core(p, lb, gnorm_g, d):
    lp = p.shape[0]
    heads = 4
    width = heads * A_HEAD_DIM
    ncol = d // width
    blk = lambda part: pl.BlockSpec((ROW_TILE, width), lambda j, r: (r, part * ncol + j))
    range_sums = _hgrn2_range_sums()
    return pl.pallas_call(
        functools.partial(_hgrn2_kernel, heads=heads, chunks=ROW_TILE // CHUNK),
        grid=(ncol, lp // ROW_TILE),
        in_specs=[blk(0), blk(1), blk(2), blk(3), pl.BlockSpec((1, width), lambda j, r: (0, j)),
                  pl.BlockSpec((1, A_HEAD_DIM), lambda j, r: (0, 0)),
                  pl.BlockSpec(range_sums.shape, lambda j, r: (0, 0))],
        out_specs=pl.BlockSpec((ROW_TILE, width), lambda j, r: (r, j)),
        out_shape=jax.ShapeDtypeStruct((lp, d), BF16),
        scratch_shapes=[pltpu.VMEM((heads, A_HEAD_DIM, A_HEAD_DIM), F32)],
        compiler_params=_cparams(2), name="hgrn2",
    )(p, p, p, p, lb.reshape(1, d), gnorm_g.reshape(1, A_HEAD_DIM), range_sums)


def _row_chunk_id(row):
    return jnp.maximum((row - (ROW0 - CHUNK)) >> int(math.log2(CHUNK)), 0)


def _attn_kernel(q_ref, k_ref, v_ref, g_ref, lam_ref, sg_ref, o_ref, qx_ref, s_ref, acc_ref, m_ref, l_ref, *,
                 tq, lambda_init):
    i = pl.program_id(1)
    q = q_ref[...]
    lane = _iota2(q.shape, 1)
    qx_ref[0:tq, :] = jnp.where(lane < B_QK_DIM, q, jnp.zeros_like(q))
    qx_ref[tq:2 * tq, :] = jnp.where(lane >= B_QK_DIM, q, jnp.zeros_like(q))
    acc_ref[...] = jnp.zeros_like(acc_ref)
    m_ref[...] = jnp.full_like(m_ref, MASK_VALUE)
    l_ref[...] = jnp.zeros_like(l_ref)

    def lane_fold(x, op):
        out = x[:, 0:128]
        for t in range(1, x.shape[1] // 128):
            out = op(out, x[:, t * 128:(t + 1) * 128])
        return out

    def lane_tile(x, n):
        return jnp.concatenate([x] * n, axis=1)

    def key_rows(row0, width):
        return pl.ds(row0 if isinstance(row0, int) else pl.multiple_of(row0, tq), width)

    def scores(row0, width):
        rows = key_rows(row0, width)
        return lax.dot_general(qx_ref[...], k_ref[rows, :], (((1,), (1,)), ((), ())), preferred_element_type=F32)

    q_chunk = _row_chunk_id(i * tq + (_iota2((2 * tq, 1), 0) & (tq - 1)))

    def mask_scores(s, row0):
        k_row = row0 + _iota2((1, s.shape[1]), 1)
        k_chunk = jnp.where(k_row >= ROW0 - N_META, _row_chunk_id(k_row), NO_CHUNK)
        return jnp.where(k_chunk <= q_chunk, s, MASK_VALUE)

    def soften(s):
        m_prev = m_ref[...]
        m_new = jnp.maximum(m_prev, jnp.max(lane_fold(s, jnp.maximum), axis=1, keepdims=True))
        alpha = jnp.exp2(m_prev - m_new)
        p = jnp.exp2(s - lane_tile(m_new, s.shape[1] // 128))
        l_ref[...] = alpha * l_ref[...] + lane_fold(p, jnp.add)
        m_ref[...] = m_new
        return p.astype(BF16), alpha

    def value_product(p, row0, width):
        return jnp.dot(p, v_ref[key_rows(row0, width), :], preferred_element_type=F32)

    def rescale_add(alpha, pv):
        acc_ref[...] = acc_ref[...] * lane_tile(alpha, B_V_DIM // 128) + pv

    first_key = (ROW0 - N_META) // 128 * 128
    head_width = tq - first_key

    def closing_step(s_last):
        row0 = jnp.maximum(i, 1) * tq
        p, alpha = soften(jnp.concatenate(
            [mask_scores(scores(first_key, head_width), first_key), mask_scores(s_last, row0)], axis=1))
        rescale_add(alpha, value_product(p[:, :head_width], first_key, head_width)
                    + value_product(p[:, head_width:], row0, tq))

    n_full = jnp.maximum(i - 1, 0)

    def tile_start(t):
        return tq * (t + 1)

    def unrolled_tiles(u, carry):
        for t in range(TILE_UNROLL):
            cur = t % 2
            s_ref[1 - cur] = scores(tile_start(TILE_UNROLL * u + t + 1), tq)
            p, alpha = soften(s_ref[cur])
            rescale_add(alpha, value_product(p, tq * (TILE_UNROLL * u + t + 1), tq))
        return carry

    def single_tile(t, carry):
        s_next = scores(tile_start(t + 1), tq)
        p, alpha = soften(s_ref[0])
        rescale_add(alpha, value_product(p, tq * (t + 1), tq))
        s_ref[0] = s_next
        return carry

    n_main = n_full // TILE_UNROLL
    s_ref[0] = scores(tile_start(0), tq)
    lax.fori_loop(0, n_main, unrolled_tiles, 0)
    lax.fori_loop(n_main * TILE_UNROLL, n_full, single_tile, 0)

    closing_step(s_ref[0])

    lam4 = lam_ref[...]
    lam = (jnp.exp(jnp.sum(lam4[0:1] * lam4[1:2], axis=1, keepdims=True))
           - jnp.exp(jnp.sum(lam4[2:3] * lam4[3:4], axis=1, keepdims=True)) + lambda_init)
    acc = acc_ref[...] / jnp.sum(l_ref[...], axis=1, keepdims=True)
    o = acc[0:tq] - lam * acc[tq:2 * tq]
    o = _rms(o, sg_ref[...], SUBLN_EPS) * (1.0 - lambda_init)
    o_ref[...] = (o * _silu(g_ref[...].astype(F32))).astype(o_ref.dtype)


def _attn_core(qk, p, lam4, subln_g, d, lambda_init):
    lp = p.shape[0]
    heads = d // B_V_DIM
    tq = ATTN_TILE
    assert ROW0 == tq and lp % tq == 0 and lp >= 2 * tq
    kernel = functools.partial(_attn_kernel, tq=tq, lambda_init=lambda_init)
    return pl.pallas_call(
        kernel, grid=(heads, lp // tq),
        in_specs=[pl.BlockSpec((tq, B_V_DIM), lambda h, i: (i, h)),
                  pl.BlockSpec((lp, B_V_DIM), lambda h, i: (0, heads + h)),
                  pl.BlockSpec((lp, B_V_DIM), lambda h, i: (0, 2 * heads + h)),
                  pl.BlockSpec((tq, B_V_DIM), lambda h, i: (i, 3 * heads + h)),
                  pl.BlockSpec((4, B_QK_DIM), lambda h, i: (0, 0)),
                  pl.BlockSpec((1, B_V_DIM), lambda h, i: (0, 0))],
        out_specs=pl.BlockSpec((tq, B_V_DIM), lambda h, i: (i, h)),
        out_shape=jax.ShapeDtypeStruct((lp, d), BF16),
        scratch_shapes=[pltpu.VMEM((2 * tq, B_V_DIM), BF16), pltpu.VMEM((2, 2 * tq, tq), F32),
                        pltpu.VMEM((2 * tq, B_V_DIM), F32),
                        pltpu.VMEM((2 * tq, 128), F32), pltpu.VMEM((2 * tq, 128), F32)],
        compiler_params=_cparams(2), name="diff_attn",
    )(qk, qk, p, p, lam4, subln_g.reshape(1, B_V_DIM))


def _rope_tables(lp):
    pos = (jnp.arange(lp) - (ROW0 - N_META)).astype(F32)
    inv_freq = ROPE_THETA ** (-jnp.arange(0, ROPE_DIM, 2, dtype=F32) / ROPE_DIM)
    ang = pos[:, None] * inv_freq[None, :]
    cos, sin = jnp.cos(ang), jnp.sin(ang)
    pad = B_QK_DIM - ROPE_DIM
    cos_t = jnp.concatenate([cos, cos, jnp.ones((lp, pad), F32)], axis=1)
    sin_t = jnp.concatenate([-sin, sin, jnp.zeros((lp, pad), F32)], axis=1)
    return cos_t, sin_t


RWKV_GROUP = 256 // C_HEAD_DIM
RWKV_GROUP_W = RWKV_GROUP * C_HEAD_DIM
RWKV_SKEW = 1


def _rwkv_kernel(r_ref, k_ref, v_ref, g_ref, dw_ref, da_ref, w0_ref, a0_ref, kk_ref, ka_ref, rk_ref,
                 lw_ref, lb_ref, o_ref, st_ref, *, groups, chunks):
    @pl.when(pl.program_id(1) == 0)
    def _():
        st_ref[...] = jnp.zeros_like(st_ref)

    n = C_HEAD_DIM
    gw = RWKV_GROUP_W
    width = groups * gw
    shift = int(math.log2(n))
    ri = _iota2((gw, gw), 0)
    ci = _iota2((gw, gw), 1)
    same = (ri >> shift) == (ci >> shift)
    seg = same.astype(BF16)

    def seg_sum(x):
        return jnp.concatenate([jnp.dot(x[:, g * gw:(g + 1) * gw].astype(BF16), seg, preferred_element_type=F32)
                                for g in range(groups)], axis=1)

    strict = same & ((ci & (n - 1)) < (ri & (n - 1)))
    eye = ri == ci
    wide_t = _iota2((CHUNK, gw), 0)
    wide_s = _iota2((CHUNK, gw), 1) & (n - 1)
    strict_w = wide_s < wide_t
    lower_w = wide_s <= wide_t

    def tile_rows(y):
        return jnp.concatenate([y] * RWKV_GROUP, axis=0)

    def expand(y):
        return jnp.where(same, tile_rows(y), 0.0)

    def compact(e):
        out = e[0:CHUNK]
        for h in range(1, RWKV_GROUP):
            out = out + e[h * CHUNK:(h + 1) * CHUNK]
        return out

    tok = {}
    per = {}

    def part(c, g, name):
        return tok[c][name][:, g * gw:(g + 1) * gw]

    def prepare(c):
        rows = slice(c * CHUNK, (c + 1) * CHUNK)
        r = r_ref[rows, :]
        k = k_ref[rows, :]
        v = v_ref[rows, :]
        logw = -C_DECAY_SCALE * jax.nn.sigmoid(w0_ref[...] + dw_ref[rows, :])
        alpha = jax.nn.sigmoid(a0_ref[...] + da_ref[rows, :])
        kk = k * kk_ref[...]
        kk = kk / jnp.maximum(jnp.sqrt(seg_sum(kk * kk)), 1e-12)
        k2 = k * (1.0 + (alpha - 1.0) * ka_ref[...])
        cum = _chunk_cumsum(logw)
        cum_last = cum[CHUNK - 1:CHUNK, :]
        b = kk * alpha
        e_inv = jnp.exp(-cum)
        e_tail = jnp.exp(cum_last - cum)
        tok[c] = dict(r=r, v=v, k2=k2, p_t=-kk * jnp.exp(cum - logw), r_t=r * jnp.exp(cum), b_t=b * e_inv,
                      k_t=k2 * e_inv, b_h=b * e_tail, k_h=k2 * e_tail, gam=jnp.exp(cum_last))
        for g in range(groups):
            x = per[c, g] = dict(ep=expand(part(c, g, "p_t")), ev=expand(part(c, g, "v")))
            x["a_all"] = _dot_nt(jnp.concatenate([part(c, g, "p_t"), part(c, g, "r_t")], axis=0),
                                 jnp.concatenate([expand(part(c, g, "b_t")), expand(part(c, g, "k_t"))], axis=0))

    def split(c):
        for g in range(groups):
            x = per[c, g]
            a_all = x.pop("a_all")
            x["power"] = jnp.where(strict, tile_rows(a_all[:CHUNK, :gw]), 0.0)
            x["t_inv"] = jnp.where(eye, 1.0, x["power"])
            x["a3"] = jnp.where(lower_w, a_all[CHUNK:, :gw], 0.0)
            x["l2v"] = expand(_dot(jnp.where(strict_w, a_all[:CHUNK, gw:], 0.0), x["ev"]))
            x["a4v"] = _dot(jnp.where(lower_w, a_all[CHUNK:, gw:], 0.0), x["ev"])

    def neumann(c):
        for g in range(groups):
            x = per[c, g]
            x["power"] = _dot(x["power"], x["power"])
            x["t_inv"] = x["t_inv"] + _dot(x["t_inv"], x["power"])

    def solve(c):
        for g in range(groups):
            x = per[c, g]
            x["wv"] = _dot(x["t_inv"], jnp.concatenate([x["ep"], x["l2v"]], axis=1))

    def finish(c):
        for g in range(groups):
            x = per[c, g]
            qo = _dot(x["a3"], x["wv"])
            x["q_t"] = part(c, g, "r_t") + qo[:, :gw]
            x["o_intra"] = qo[:, gw:] + x["a4v"]
            x["w_c"] = compact(x["wv"][:, :gw])
            x["v_c"] = compact(x["wv"][:, gw:])
            x["bk"] = jnp.concatenate([part(c, g, "b_h"), part(c, g, "k_h")], axis=0)
            x["gam_col"] = jnp.sum(jnp.where(eye, part(c, g, "gam"), 0.0), axis=1, keepdims=True)

    def read_state(c):
        for g in range(groups):
            x = per[c, g]
            st = st_ref[g]
            x["o"] = _dot(x["q_t"], st) + x["o_intra"]
            x["u_c"] = _dot(x["w_c"], st) + x["v_c"]
            x["st_decayed"] = st * x["gam_col"]

    def write_state(c):
        rows = slice(c * CHUNK, (c + 1) * CHUNK)
        for g in range(groups):
            x = per[c, g]
            update = _dot_tn(x["bk"], jnp.concatenate([x["u_c"], part(c, g, "v")], axis=0))
            st_ref[g] = x["st_decayed"] + jnp.where(same, update, 0.0)
        o = jnp.concatenate([per[c, g]["o"] for g in range(groups)], axis=1)
        mean = seg_sum(o) * (1.0 / n)
        cen = o - mean
        var = seg_sum(cen * cen) * (1.0 / n)
        o = cen * lax.rsqrt(var + LNX_EPS) * lw_ref[...] + lb_ref[...]
        o = o + seg_sum(tok[c]["r"] * tok[c]["k2"] * rk_ref[...]) * tok[c]["v"]
        o_ref[rows, :] = (o * _silu(g_ref[rows, :])).astype(o_ref.dtype)

    stages = [prepare, split] + [neumann] * (int(math.log2(CHUNK)) - 1) + [solve, finish, read_state, write_state]
    for tick in range(len(stages) + RWKV_SKEW * (chunks - 1)):
        for c in range(chunks):
            s = tick - RWKV_SKEW * c
            if 0 <= s < len(stages):
                stages[s](c)


def _rwkv_core(r, k, v, gate, dw, da, vecs, d):
    lp = r.shape[0]
    groups = min(4, d // RWKV_GROUP_W)
    chunks = 4
    width = groups * RWKV_GROUP_W
    rows = chunks * CHUNK
    act = pl.BlockSpec((rows, width), lambda j, c: (c, j))
    vec = pl.BlockSpec((1, width), lambda j, c: (0, j))
    return pl.pallas_call(
        functools.partial(_rwkv_kernel, groups=groups, chunks=chunks), grid=(d // width, lp // rows),
        in_specs=[act] * 6 + [vec] * 7, out_specs=act,
        out_shape=jax.ShapeDtypeStruct((lp, d), BF16),
        scratch_shapes=[pltpu.VMEM((groups, RWKV_GROUP_W, RWKV_GROUP_W), F32)],
        compiler_params=_cparams(2), name="rwkv7",
    )(r, k, v, gate, dw, da, *[t.reshape(1, d) for t in vecs])


def _hgrn2_layer(u, w_in, j, lb, gnorm_g, w_out, d):
    p = _matmul(u, w_in, j, 4 * d)
    return _matmul(_hgrn2_core(p, lb, gnorm_g, d), w_out, j, d, out_dtype=BF16)


def _attn_layer(u, w_in, j, lam4, subln_g, w_out, d, lambda_init, tables):
    p = _matmul(u, w_in, j, 4 * d, out_dtype=BF16)
    qk = _rope_qk(p, *tables, d)
    return _matmul(_attn_core(qk, p, lam4, subln_g, d, lambda_init), w_out, j, d)


def _rwkv_layer(xs, w_in, j, w0, w1, w2, a0, a1, a2, k_k, k_a, r_k, lnx_w, lnx_b, w_out, d):
    r, k, v, gate = (_matmul(xs, w_in, j, d, col_off=s * d, lead=s) for s in range(4))
    dw = _lora(xs, 4, w1.astype(BF16), w2.astype(BF16), True)
    da = _lora(xs, 5, a1.astype(BF16), a2.astype(BF16), False)
    o = _rwkv_core(r, k, v, gate, dw, da, (w0, a0, k_k, k_a, r_k.reshape(-1), lnx_w, lnx_b), d)
    return _matmul(o, w_out, j, d, out_dtype=BF16)


def kernel(x, meta_tokens, pre_norm_g, post_norm_g, a_w_in, a_lb_logits, a_gnorm_g, a_w_out, b_w_in, b_lam_q1, b_lam_k1, b_lam_q2, b_lam_k2, b_subln_g, b_w_out, c_mu, c_w_in, c_w0, c_w1, c_w2, c_a0, c_a1, c_a2, c_k_k, c_k_a, c_r_k, c_lnx_w, c_lnx_b, c_w_out):
    bsz, seq, d = x.shape
    depth = pre_norm_g.shape[0]
    assert bsz == 1 and seq % ATTN_TILE == 0 and meta_tokens.shape[0] == N_META and depth >= 2
    lp = ROW0 + seq
    h = (jnp.concatenate([jnp.zeros((ROW0 - N_META, d), F32), meta_tokens.astype(F32)], axis=0), x[0])
    tables = _rope_tables(lp)
    lb_p = jax.nn.softmax(a_lb_logits.astype(F32), axis=0)
    lb_all = jnp.cumsum(lb_p, axis=0) - lb_p[0]

    u = _prenorm(h, pre_norm_g[0], BF16)
    for i in range(depth):
        kind, j = i % N_MIXERS, i // N_MIXERS
        if kind == 0:
            y = _hgrn2_layer(u, a_w_in, j, lb_all[j], a_gnorm_g[j], a_w_out, d)
        elif kind == 1:
            lambda_init = 0.8 - 0.6 * math.exp(-0.3 * i)
            lam4 = jnp.stack([b_lam_q1[j], b_lam_k1[j], b_lam_q2[j], b_lam_k2[j]]).astype(F32)
            y = _attn_layer(u, b_w_in, j, lam4, b_subln_g[j], b_w_out, d, lambda_init, tables)
        else:
            y = _rwkv_layer(u, c_w_in, j, c_w0[j], c_w1[j], c_w2[j], c_a0[j], c_a1[j], c_a2[j],
                            c_k_k[j], c_k_a[j], c_r_k[j], c_lnx_w[j], c_lnx_b[j], c_w_out, d)
        if i + 1 == depth:
            out = _residual_out(h[0], y, post_norm_g[i], seq)
        elif (i + 1) % N_MIXERS == 2:
            hn, u = _residual_shiftmix(h[0], y, post_norm_g[i], pre_norm_g[i + 1], c_mu[(i + 1) // N_MIXERS])
            h = (hn,)
        else:
            hn, u = _residual_prenorm(h, y, post_norm_g[i], pre_norm_g[i + 1], BF16)
            h = (hn,)
    return out[None]
```

```python
import functools
import math

import jax
import jax.numpy as jnp
import numpy as np
from jax import lax
from jax.experimental import pallas as pl
from jax.experimental.pallas import tpu as pltpu

F32 = jnp.float32
BF16 = jnp.bfloat16

CHUNK = 64
N_META = 16
N_MIXERS = 3
NORM_EPS = 1e-6
A_HEAD_DIM = 128
B_QK_DIM = 128
B_V_DIM = 2 * B_QK_DIM
ROPE_THETA = 500000.0
ROPE_DIM = B_QK_DIM // 4
SUBLN_EPS = 1e-5
C_HEAD_DIM = 64
C_DECAY_SCALE = 0.606531
LNX_EPS = 64e-5

ROW0 = 512
ROW_TILE = 256
ATTN_TILE = 512
MASK_VALUE = -1e30
NO_CHUNK = 1 << 30
TILE_UNROLL = 4
Q_SCALE_LOG2 = B_QK_DIM ** -0.5 * math.log2(math.e)
V7X_VMEM_LIMIT = 56 * 1024 * 1024


def _cparams(n_axes, vmem=V7X_VMEM_LIMIT):
    return pltpu.CompilerParams(dimension_semantics=("arbitrary",) * n_axes, vmem_limit_bytes=vmem)


def _dot(a, b):
    return jnp.dot(a.astype(BF16), b.astype(BF16), preferred_element_type=F32)


def _dot_nt(a, b):
    return lax.dot_general(a.astype(BF16), b.astype(BF16), (((1,), (1,)), ((), ())), preferred_element_type=F32)


def _dot_tn(a, b):
    return lax.dot_general(a.astype(BF16), b.astype(BF16), (((0,), (0,)), ((), ())), preferred_element_type=F32)


def _iota2(shape, axis):
    return lax.broadcasted_iota(jnp.int32, shape, axis)


def _chunk_cumsum(x):
    tri = (_iota2((CHUNK, CHUNK), 1) <= _iota2((CHUNK, CHUNK), 0)).astype(BF16)
    hi = x.astype(BF16)
    lo = (x - hi.astype(F32)).astype(BF16)
    return jnp.dot(tri, hi, preferred_element_type=F32) + jnp.dot(tri, lo, preferred_element_type=F32)


def _silu(x):
    return x * jax.nn.sigmoid(x)


def _rms(x, g, eps):
    return x * lax.rsqrt(jnp.mean(x * x, axis=-1, keepdims=True) + eps) * g


HEAD_TILES = ROW0 // ROW_TILE


def _stream_rows(refs):
    if len(refs) == 1:
        return refs[0][...]
    head_ref, frames_ref = refs
    return jnp.where(pl.program_id(0) < HEAD_TILES, head_ref[...], frames_ref[...])


def _stream_specs(h):
    d = h[0].shape[1]
    if len(h) == 1:
        return [pl.BlockSpec((ROW_TILE, d), lambda i: (i, 0))]
    return [pl.BlockSpec((ROW_TILE, d), lambda i: (jnp.minimum(i, HEAD_TILES - 1), 0)),
            pl.BlockSpec((ROW_TILE, d), lambda i: (jnp.maximum(i - HEAD_TILES, 0), 0))]


def _prenorm_kernel(*refs, n_h):
    g_ref, u_ref = refs[n_h:]
    u_ref[...] = _rms(_stream_rows(refs[:n_h]), g_ref[...], NORM_EPS).astype(u_ref.dtype)


def _residual_prenorm_kernel(*refs, n_h):
    y_ref, gp_ref, g_ref, hn_ref, u_ref = refs[n_h:]
    hn = _stream_rows(refs[:n_h]) + _rms(y_ref[...].astype(F32), gp_ref[...], NORM_EPS)
    hn_ref[...] = hn
    u_ref[...] = _rms(hn, g_ref[...], NORM_EPS).astype(u_ref.dtype)


def _residual_shiftmix_kernel(h_ref, y_ref, hp_ref, yp_ref, gp_ref, g_ref, mu_ref, hn_ref, xs_ref):
    hn = h_ref[...] + _rms(y_ref[...].astype(F32), gp_ref[...], NORM_EPS)
    hn_ref[...] = hn
    u = _rms(hn, g_ref[...], NORM_EPS)
    last = _rms(hp_ref[7:8, :] + _rms(yp_ref[7:8, :], gp_ref[...], NORM_EPS), g_ref[...], NORM_EPS)
    last = jnp.where(pl.program_id(0) == 0, jnp.zeros_like(last), last)
    prev = jnp.where(_iota2(u.shape, 0) == 0, last, pltpu.roll(u, 1, 0))
    delta = prev - u
    for s in range(xs_ref.shape[0]):
        xs_ref[s] = (u + delta * mu_ref[s:s + 1, :]).astype(xs_ref.dtype)


def _residual_kernel(h_ref, y_ref, gp_ref, o_ref):
    o_ref[...] = h_ref[...] + _rms(y_ref[...].astype(F32), gp_ref[...], NORM_EPS)


def _prenorm(h, g, u_dtype):
    d = g.shape[0]
    lp = h[0].shape[0] if len(h) == 1 else ROW0 + h[1].shape[0]
    row = pl.BlockSpec((ROW_TILE, d), lambda i: (i, 0))
    vec = pl.BlockSpec((1, d), lambda i: (0, 0))
    return pl.pallas_call(
        functools.partial(_prenorm_kernel, n_h=len(h)), grid=(lp // ROW_TILE,),
        in_specs=_stream_specs(h) + [vec], out_specs=row,
        out_shape=jax.ShapeDtypeStruct((lp, d), u_dtype), compiler_params=_cparams(1), name="prenorm",
    )(*h, g.reshape(1, d))


def _residual_prenorm(h, y, gp, g, u_dtype):
    lp, d = y.shape
    row = pl.BlockSpec((ROW_TILE, d), lambda i: (i, 0))
    vec = pl.BlockSpec((1, d), lambda i: (0, 0))
    return pl.pallas_call(
        functools.partial(_residual_prenorm_kernel, n_h=len(h)), grid=(lp // ROW_TILE,),
        in_specs=_stream_specs(h) + [row, vec, vec], out_specs=[row, row],
        out_shape=[jax.ShapeDtypeStruct((lp, d), F32), jax.ShapeDtypeStruct((lp, d), u_dtype)],
        compiler_params=_cparams(1), name="residual_prenorm",
    )(*h, y, gp.reshape(1, d), g.reshape(1, d))


def _residual_shiftmix(h, y, gp, g, mu):
    lp, d = y.shape
    n = mu.shape[0]
    tile = ROW_TILE // 2
    per8 = tile // 8
    row = pl.BlockSpec((tile, d), lambda i: (i, 0))
    before = pl.BlockSpec((8, d), lambda i: (jnp.maximum(i * per8 - 1, 0), 0))
    vec = pl.BlockSpec((1, d), lambda i: (0, 0))
    return pl.pallas_call(
        _residual_shiftmix_kernel, grid=(lp // tile,),
        in_specs=[row, row, before, before, vec, vec, pl.BlockSpec((n, d), lambda i: (0, 0))],
        out_specs=[row, pl.BlockSpec((n, tile, d), lambda i: (0, i, 0))],
        out_shape=[jax.ShapeDtypeStruct((lp, d), F32), jax.ShapeDtypeStruct((n, lp, d), BF16)],
        compiler_params=_cparams(1), name="residual_shiftmix",
    )(h, y, h, y, gp.reshape(1, d), g.reshape(1, d), mu)


def _residual_out(h, y, gp, seq):
    lp, d = h.shape
    off = ROW0 // ROW_TILE
    row_in = pl.BlockSpec((ROW_TILE, d), lambda i: (i + off, 0))
    vec = pl.BlockSpec((1, d), lambda i: (0, 0))
    return pl.pallas_call(
        _residual_kernel, grid=(seq // ROW_TILE,), in_specs=[row_in, row_in, vec],
        out_specs=pl.BlockSpec((ROW_TILE, d), lambda i: (i, 0)),
        out_shape=jax.ShapeDtypeStruct((seq, d), F32), compiler_params=_cparams(1), name="residual_out",
    )(h, y, gp.reshape(1, d))


def _mm_kernel(a_ref, w_ref, o_ref, wb_ref):
    @pl.when(pl.program_id(1) == 0)
    def _():
        wb_ref[...] = w_ref[...].astype(BF16)

    o_ref[...] = jnp.dot(a_ref[...], wb_ref[...], preferred_element_type=F32).astype(o_ref.dtype)


def _row_block(m, cap=1536):
    n = m // ROW_TILE
    best = max(d for d in range(1, n + 1) if n % d == 0 and ROW_TILE * d <= cap)
    return ROW_TILE * best


def _lhs_spec(a, tm, lead, index_map):
    k = a.shape[-1]
    if a.ndim == 2:
        return pl.BlockSpec((tm, k), lambda *g: (index_map(*g), 0))
    return pl.BlockSpec((None, tm, k), lambda *g: (lead, index_map(*g), 0))


def _matmul(a, w, layer, n_out, col_off=0, lead=0, out_dtype=F32, tn=512):
    m = a.shape[-2]
    k = a.shape[-1]
    tm = _row_block(m)
    tn = min(tn, n_out)
    joff = col_off // tn
    return pl.pallas_call(
        _mm_kernel, grid=(n_out // tn, m // tm),
        in_specs=[_lhs_spec(a, tm, lead, lambda j, i: i),
                  pl.BlockSpec((None, k, tn), lambda j, i: (layer, 0, j + joff))],
        out_specs=pl.BlockSpec((tm, tn), lambda j, i: (i, j)),
        out_shape=jax.ShapeDtypeStruct((m, n_out), out_dtype),
        scratch_shapes=[pltpu.VMEM((k, tn), BF16)],
        compiler_params=_cparams(2), name="matmul",
    )(a, w)


def _rope(x, cos, sin):
    first = _iota2(x.shape, 1) < ROPE_DIM // 2
    other = jnp.where(first, pltpu.roll(x, B_QK_DIM - ROPE_DIM // 2, 1), pltpu.roll(x, ROPE_DIM // 2, 1))
    return x * cos + other * sin


def _rope_kernel(p_ref, cos_ref, sin_ref, o_ref):
    fac = jnp.where(pl.program_id(1) == 0, Q_SCALE_LOG2, 1.0)
    cos = cos_ref[...] * fac
    sin = sin_ref[...] * fac
    for c in range(p_ref.shape[1] // B_QK_DIM):
        cols = slice(c * B_QK_DIM, (c + 1) * B_QK_DIM)
        o_ref[:, cols] = _rope(p_ref[:, cols].astype(F32), cos, sin).astype(o_ref.dtype)


def _rope_qk(p, cos_t, sin_t, d):
    lp = p.shape[0]
    table = pl.BlockSpec((ROW_TILE, B_QK_DIM), lambda r, c: (r, 0))
    return pl.pallas_call(
        _rope_kernel, grid=(lp // ROW_TILE, 2),
        in_specs=[pl.BlockSpec((ROW_TILE, d), lambda r, c: (r, c)), table, table],
        out_specs=pl.BlockSpec((ROW_TILE, d), lambda r, c: (r, c)),
        out_shape=jax.ShapeDtypeStruct((lp, 2 * d), BF16), compiler_params=_cparams(2), name="rope_qk",
    )(p, cos_t, sin_t)


def _lora_kernel(x_ref, a_ref, b_ref, o_ref, *, use_tanh):
    mid = jnp.dot(x_ref[...], a_ref[...], preferred_element_type=F32)
    if use_tanh:
        mid = jnp.tanh(mid)
    o_ref[...] = _dot(mid, b_ref[...]).astype(o_ref.dtype)


def _lora(x, lead, a, b, use_tanh):
    m, d = x.shape[-2:]
    r = a.shape[1]
    return pl.pallas_call(
        functools.partial(_lora_kernel, use_tanh=use_tanh), grid=(m // ROW_TILE,),
        in_specs=[_lhs_spec(x, ROW_TILE, lead, lambda i: i), pl.BlockSpec((d, r), lambda i: (0, 0)),
                  pl.BlockSpec((r, d), lambda i: (0, 0))],
        out_specs=pl.BlockSpec((ROW_TILE, d), lambda i: (i, 0)),
        out_shape=jax.ShapeDtypeStruct((m, d), BF16), compiler_params=_cparams(1), name="lora",
    )(x, a, b)


HGRN2_LEVELS = tuple(CHUNK >> (i + 1) for i in range(int(math.log2(CHUNK))))


def _hgrn2_range_sums():
    t = np.arange(CHUNK)[:, None]
    c = np.arange(CHUNK)[None, :]
    blocks = [c <= t]
    for h in HGRN2_LEVELS:
        r = (t // (2 * h)) * 2 * h + h
        blocks.append(np.where(t >= r, (c > r) & (c <= t), (c > t) & (c <= r)))
    return jnp.asarray(np.concatenate(blocks, axis=0), dtype=BF16)


def _hgrn2_kernel(q_ref, f_ref, i_ref, g_ref, lb_ref, gn_ref, rs_ref, o_ref, st_ref, *, heads, chunks):
    @pl.when(pl.program_id(1) == 0)
    def _():
        st_ref[...] = jnp.zeros_like(st_ref)

    dh = A_HEAD_DIM
    width = heads * dh
    n_lev = len(HGRN2_LEVELS)
    ti = _iota2((CHUNK, CHUNK), 0)
    si = _iota2((CHUNK, CHUNK), 1)
    eye = ti == si
    pair_masks = []
    for h in HGRN2_LEVELS:
        sh = int(math.log2(h))
        pair_masks.append((((ti >> sh) & 1) == 1) & ((ti >> (sh + 1)) == (si >> (sh + 1))) & (((si >> sh) & 1) == 0))
    trow = _iota2((CHUNK, width), 0)
    upper = [((trow >> int(math.log2(h))) & 1) == 1 for h in HGRN2_LEVELS]
    lb = lb_ref[...]
    head_cols = [slice(h * dh, (h + 1) * dh) for h in range(heads)]

    tok = []
    for c in range(chunks):
        rows = slice(c * CHUNK, (c + 1) * CHUNK)
        q = q_ref[rows, :]
        v = i_ref[rows, :]
        f = lb + (1.0 - lb) * jax.nn.sigmoid(f_ref[rows, :])
        g = jnp.log(f)
        k = 1.0 - f
        g_hi = g.astype(BF16)
        g_lo = (g - g_hi.astype(F32)).astype(BF16)
        e2 = jnp.dot(rs_ref[...], jnp.concatenate([g_hi, g_lo], axis=1), preferred_element_type=F32)
        e = e2[:, :width] + e2[:, width:]
        b = e[0:CHUNK]
        b_last = b[CHUNK - 1:CHUNK, :]
        ex = jnp.exp(e)
        xs = [jnp.where(upper[i], q, k) * ex[(i + 1) * CHUNK:(i + 2) * CHUNK] for i in range(n_lev)]
        tok.append(dict(q=q, k=k, v=v, qe=q * ex[0:CHUNK], kd=k * jnp.exp(b_last - b), decay=jnp.exp(b_last), xs=xs))

    pairs = [(c, cs) for c in range(chunks) for cs in head_cols]
    scores = [jnp.where(eye, _dot_nt(tok[c]["q"][:, cs], tok[c]["k"][:, cs]), 0.0) for c, cs in pairs]
    for i in range(n_lev):
        scores = [sc + jnp.where(pair_masks[i], _dot_nt(tok[c]["xs"][i][:, cs], tok[c]["xs"][i][:, cs]), 0.0)
                  for sc, (c, cs) in zip(scores, pairs)]
    intra = [_dot(sc, tok[c]["v"][:, cs]) for sc, (c, cs) in zip(scores, pairs)]
    kv = [_dot_tn(tok[c]["v"][:, cs], tok[c]["kd"][:, cs]) for c, cs in pairs]
    for n, (c, cs) in enumerate(pairs):
        h = n % heads
        rows = slice(c * CHUNK, (c + 1) * CHUNK)
        st = st_ref[h]
        o = _dot_nt(tok[c]["qe"][:, cs], st) + intra[n]
        st_ref[h] = st * tok[c]["decay"][:, cs] + kv[n]
        o = _rms(o, gn_ref[...], NORM_EPS)
        o_ref[rows, cs] = (o * _silu(g_ref[rows, cs])).astype(o_ref.dtype)


def _hgrn2_core(p, lb, gnorm_g, d):
    lp = p.shape[0]
    heads = 4
    width = heads * A_HEAD_DIM
    ncol = d // width
    blk = lambda part: pl.BlockSpec((ROW_TILE, width), lambda j, r: (r, part * ncol + j))
    range_sums = _hgrn2_range_sums()
    return pl.pallas_call(
        functools.partial(_hgrn2_kernel, heads=heads, chunks=ROW_TILE // CHUNK),
        grid=(ncol, lp // ROW_TILE),
        in_specs=[blk(0), blk(1), blk(2), blk(3), pl.BlockSpec((1, width), lambda j, r: (0, j)),
                  pl.BlockSpec((1, A_HEAD_DIM), lambda j, r: (0, 0)),
                  pl.BlockSpec(range_sums.shape, lambda j, r: (0, 0))],
        out_specs=pl.BlockSpec((ROW_TILE, width), lambda j, r: (r, j)),
        out_shape=jax.ShapeDtypeStruct((lp, d), BF16),
        scratch_shapes=[pltpu.VMEM((heads, A_HEAD_DIM, A_HEAD_DIM), F32)],
        compiler_params=_cparams(2), name="hgrn2",
    )(p, p, p, p, lb.reshape(1, d), gnorm_g.reshape(1, A_HEAD_DIM), range_sums)


def _row_chunk_id(row):
    return jnp.maximum((row - (ROW0 - CHUNK)) >> int(math.log2(CHUNK)), 0)


def _attn_kernel(q_ref, k_ref, v_ref, g_ref, lam_ref, sg_ref, o_ref, qx_ref, s_ref, acc_ref, m_ref, l_ref, *,
                 tq, lambda_init):
    i = pl.program_id(1)
    q = q_ref[...]
    lane = _iota2(q.shape, 1)
    qx_ref[0:tq, :] = jnp.where(lane < B_QK_DIM, q, jnp.zeros_like(q))
    qx_ref[tq:2 * tq, :] = jnp.where(lane >= B_QK_DIM, q, jnp.zeros_like(q))
    acc_ref[...] = jnp.zeros_like(acc_ref)
    m_ref[...] = jnp.full_like(m_ref, MASK_VALUE)
    l_ref[...] = jnp.zeros_like(l_ref)

    def lane_fold(x, op):
        out = x[:, 0:128]
        for t in range(1, x.shape[1] // 128):
            out = op(out, x[:, t * 128:(t + 1) * 128])
        return out

    def lane_tile(x, n):
        return jnp.concatenate([x] * n, axis=1)

    def key_rows(row0, width):
        return pl.ds(row0 if isinstance(row0, int) else pl.multiple_of(row0, tq), width)

    def scores(row0, width):
        rows = key_rows(row0, width)
        return lax.dot_general(qx_ref[...], k_ref[rows, :], (((1,), (1,)), ((), ())), preferred_element_type=F32)

    q_chunk = _row_chunk_id(i * tq + (_iota2((2 * tq, 1), 0) & (tq - 1)))

    def mask_scores(s, row0):
        k_row = row0 + _iota2((1, s.shape[1]), 1)
        k_chunk = jnp.where(k_row >= ROW0 - N_META, _row_chunk_id(k_row), NO_CHUNK)
        return jnp.where(k_chunk <= q_chunk, s, MASK_VALUE)

    def soften(s):
        m_prev = m_ref[...]
        m_new = jnp.maximum(m_prev, jnp.max(lane_fold(s, jnp.maximum), axis=1, keepdims=True))
        alpha = jnp.exp2(m_prev - m_new)
        p = jnp.exp2(s - lane_tile(m_new, s.shape[1] // 128))
        l_ref[...] = alpha * l_ref[...] + lane_fold(p, jnp.add)
        m_ref[...] = m_new
        return p.astype(BF16), alpha

    def value_product(p, row0, width):
        return jnp.dot(p, v_ref[key_rows(row0, width), :], preferred_element_type=F32)

    def rescale_add(alpha, pv):
        acc_ref[...] = acc_ref[...] * lane_tile(alpha, B_V_DIM // 128) + pv

    first_key = (ROW0 - N_META) // 128 * 128
    head_width = tq - first_key

    def closing_step(s_last):
        row0 = jnp.maximum(i, 1) * tq
        p, alpha = soften(jnp.concatenate(
            [mask_scores(scores(first_key, head_width), first_key), mask_scores(s_last, row0)], axis=1))
        rescale_add(alpha, value_product(p[:, :head_width], first_key, head_width)
                    + value_product(p[:, head_width:], row0, tq))

    n_full = jnp.maximum(i - 1, 0)

    def tile_start(t):
        return tq * (t + 1)

    def unrolled_tiles(u, carry):
        for t in range(TILE_UNROLL):
            cur = t % 2
            s_ref[1 - cur] = scores(tile_start(TILE_UNROLL * u + t + 1), tq)
            p, alpha = soften(s_ref[cur])
            rescale_add(alpha, value_product(p, tq * (TILE_UNROLL * u + t + 1), tq))
        return carry

    def single_tile(t, carry):
        s_next = scores(tile_start(t + 1), tq)
        p, alpha = soften(s_ref[0])
        rescale_add(alpha, value_product(p, tq * (t + 1), tq))
        s_ref[0] = s_next
        return carry

    n_main = n_full // TILE_UNROLL
    s_ref[0] = scores(tile_start(0), tq)
    lax.fori_loop(0, n_main, unrolled_tiles, 0)
    lax.fori_loop(n_main * TILE_UNROLL, n_full, single_tile, 0)

    closing_step(s_ref[0])

    lam4 = lam_ref[...]
    lam = (jnp.exp(jnp.sum(lam4[0:1] * lam4[1:2], axis=1, keepdims=True))
           - jnp.exp(jnp.sum(lam4[2:3] * lam4[3:4], axis=1, keepdims=True)) + lambda_init)
    acc = acc_ref[...] / jnp.sum(l_ref[...], axis=1, keepdims=True)
    o = acc[0:tq] - lam * acc[tq:2 * tq]
    o = _rms(o, sg_ref[...], SUBLN_EPS) * (1.0 - lambda_init)
    o_ref[...] = (o * _silu(g_ref[...].astype(F32))).astype(o_ref.dtype)


def _attn_core(qk, p, lam4, subln_g, d, lambda_init):
    lp = p.shape[0]
    heads = d // B_V_DIM
    tq = ATTN_TILE
    assert ROW0 == tq and lp % tq == 0 and lp >= 2 * tq
    kernel = functools.partial(_attn_kernel, tq=tq, lambda_init=lambda_init)
    return pl.pallas_call(
        kernel, grid=(heads, lp // tq),
        in_specs=[pl.BlockSpec((tq, B_V_DIM), lambda h, i: (i, h)),
                  pl.BlockSpec((lp, B_V_DIM), lambda h, i: (0, heads + h)),
                  pl.BlockSpec((lp, B_V_DIM), lambda h, i: (0, 2 * heads + h)),
                  pl.BlockSpec((tq, B_V_DIM), lambda h, i: (i, 3 * heads + h)),
                  pl.BlockSpec((4, B_QK_DIM), lambda h, i: (0, 0)),
                  pl.BlockSpec((1, B_V_DIM), lambda h, i: (0, 0))],
        out_specs=pl.BlockSpec((tq, B_V_DIM), lambda h, i: (i, h)),
        out_shape=jax.ShapeDtypeStruct((lp, d), BF16),
        scratch_shapes=[pltpu.VMEM((2 * tq, B_V_DIM), BF16), pltpu.VMEM((2, 2 * tq, tq), F32),
                        pltpu.VMEM((2 * tq, B_V_DIM), F32),
                        pltpu.VMEM((2 * tq, 128), F32), pltpu.VMEM((2 * tq, 128), F32)],
        compiler_params=_cparams(2), name="diff_attn",
    )(qk, qk, p, p, lam4, subln_g.reshape(1, B_V_DIM))


def _rope_tables(lp):
    pos = (jnp.arange(lp) - (ROW0 - N_META)).astype(F32)
    inv_freq = ROPE_THETA ** (-jnp.arange(0, ROPE_DIM, 2, dtype=F32) / ROPE_DIM)
    ang = pos[:, None] * inv_freq[None, :]
    cos, sin = jnp.cos(ang), jnp.sin(ang)
    pad = B_QK_DIM - ROPE_DIM
    cos_t = jnp.concatenate([cos, cos, jnp.ones((lp, pad), F32)], axis=1)
    sin_t = jnp.concatenate([-sin, sin, jnp.zeros((lp, pad), F32)], axis=1)
    return cos_t, sin_t


RWKV_GROUP = 256 // C_HEAD_DIM
RWKV_GROUP_W = RWKV_GROUP * C_HEAD_DIM
RWKV_SKEW = 1


def _rwkv_kernel(r_ref, k_ref, v_ref, g_ref, dw_ref, da_ref, w0_ref, a0_ref, kk_ref, ka_ref, rk_ref,
                 lw_ref, lb_ref, o_ref, st_ref, *, groups, chunks):
    @pl.when(pl.program_id(1) == 0)
    def _():
        st_ref[...] = jnp.zeros_like(st_ref)

    n = C_HEAD_DIM
    gw = RWKV_GROUP_W
    width = groups * gw
    shift = int(math.log2(n))
    ri = _iota2((gw, gw), 0)
    ci = _iota2((gw, gw), 1)
    same = (ri >> shift) == (ci >> shift)
    seg = same.astype(BF16)

    def seg_sum(x):
        return jnp.concatenate([jnp.dot(x[:, g * gw:(g + 1) * gw].astype(BF16), seg, preferred_element_type=F32)
                                for g in range(groups)], axis=1)

    strict = same & ((ci & (n - 1)) < (ri & (n - 1)))
    eye = ri == ci
    wide_t = _iota2((CHUNK, gw), 0)
    wide_s = _iota2((CHUNK, gw), 1) & (n - 1)
    strict_w = wide_s < wide_t
    lower_w = wide_s <= wide_t

    def tile_rows(y):
        return jnp.concatenate([y] * RWKV_GROUP, axis=0)

    def expand(y):
        return jnp.where(same, tile_rows(y), 0.0)

    def compact(e):
        out = e[0:CHUNK]
        for h in range(1, RWKV_GROUP):
            out = out + e[h * CHUNK:(h + 1) * CHUNK]
        return out

    tok = {}
    per = {}

    def part(c, g, name):
        return tok[c][name][:, g * gw:(g + 1) * gw]

    def prepare(c):
        rows = slice(c * CHUNK, (c + 1) * CHUNK)
        r = r_ref[rows, :]
        k = k_ref[rows, :]
        v = v_ref[rows, :]
        logw = -C_DECAY_SCALE * jax.nn.sigmoid(w0_ref[...] + dw_ref[rows, :])
        alpha = jax.nn.sigmoid(a0_ref[...] + da_ref[rows, :])
        kk = k * kk_ref[...]
        kk = kk / jnp.maximum(jnp.sqrt(seg_sum(kk * kk)), 1e-12)
        k2 = k * (1.0 + (alpha - 1.0) * ka_ref[...])
        cum = _chunk_cumsum(logw)
        cum_last = cum[CHUNK - 1:CHUNK, :]
        b = kk * alpha
        e_inv = jnp.exp(-cum)
        e_tail = jnp.exp(cum_last - cum)
        tok[c] = dict(r=r, v=v, k2=k2, p_t=-kk * jnp.exp(cum - logw), r_t=r * jnp.exp(cum), b_t=b * e_inv,
                      k_t=k2 * e_inv, b_h=b * e_tail, k_h=k2 * e_tail, gam=jnp.exp(cum_last))
        for g in range(groups):
            x = per[c, g] = dict(ep=expand(part(c, g, "p_t")), ev=expand(part(c, g, "v")))
            x["a_all"] = _dot_nt(jnp.concatenate([part(c, g, "p_t"), part(c, g, "r_t")], axis=0),
                                 jnp.concatenate([expand(part(c, g, "b_t")), expand(part(c, g, "k_t"))], axis=0))

    def split(c):
        for g in range(groups):
            x = per[c, g]
            a_all = x.pop("a_all")
            x["power"] = jnp.where(strict, tile_rows(a_all[:CHUNK, :gw]), 0.0)
            x["t_inv"] = jnp.where(eye, 1.0, x["power"])
            x["a3"] = jnp.where(lower_w, a_all[CHUNK:, :gw], 0.0)
            x["l2v"] = expand(_dot(jnp.where(strict_w, a_all[:CHUNK, gw:], 0.0), x["ev"]))
            x["a4v"] = _dot(jnp.where(lower_w, a_all[CHUNK:, gw:], 0.0), x["ev"])

    def neumann(c):
        for g in range(groups):
            x = per[c, g]
            x["power"] = _dot(x["power"], x["power"])
            x["t_inv"] = x["t_inv"] + _dot(x["t_inv"], x["power"])

    def solve(c):
        for g in range(groups):
            x = per[c, g]
            x["wv"] = _dot(x["t_inv"], jnp.concatenate([x["ep"], x["l2v"]], axis=1))

    def finish(c):
        for g in range(groups):
            x = per[c, g]
            qo = _dot(x["a3"], x["wv"])
            x["q_t"] = part(c, g, "r_t") + qo[:, :gw]
            x["o_intra"] = qo[:, gw:] + x["a4v"]
            x["w_c"] = compact(x["wv"][:, :gw])
            x["v_c"] = compact(x["wv"][:, gw:])
            x["bk"] = jnp.concatenate([part(c, g, "b_h"), part(c, g, "k_h")], axis=0)
            x["gam_col"] = jnp.sum(jnp.where(eye, part(c, g, "gam"), 0.0), axis=1, keepdims=True)

    def read_state(c):
        for g in range(groups):
            x = per[c, g]
            st = st_ref[g]
            x["o"] = _dot(x["q_t"], st) + x["o_intra"]
            x["u_c"] = _dot(x["w_c"], st) + x["v_c"]
            x["st_decayed"] = st * x["gam_col"]

    def write_state(c):
        rows = slice(c * CHUNK, (c + 1) * CHUNK)
        for g in range(groups):
            x = per[c, g]
            update = _dot_tn(x["bk"], jnp.concatenate([x["u_c"], part(c, g, "v")], axis=0))
            st_ref[g] = x["st_decayed"] + jnp.where(same, update, 0.0)
        o = jnp.concatenate([per[c, g]["o"] for g in range(groups)], axis=1)
        mean = seg_sum(o) * (1.0 / n)
        cen = o - mean
        var = seg_sum(cen * cen) * (1.0 / n)
        o = cen * lax.rsqrt(var + LNX_EPS) * lw_ref[...] + lb_ref[...]
        o = o + seg_sum(tok[c]["r"] * tok[c]["k2"] * rk_ref[...]) * tok[c]["v"]
        o_ref[rows, :] = (o * _silu(g_ref[rows, :])).astype(o_ref.dtype)

    stages = [prepare, split] + [neumann] * (int(math.log2(CHUNK)) - 1) + [solve, finish, read_state, write_state]
    for tick in range(len(stages) + RWKV_SKEW * (chunks - 1)):
        for c in range(chunks):
            s = tick - RWKV_SKEW * c
            if 0 <= s < len(stages):
                stages[s](c)


def _rwkv_core(r, k, v, gate, dw, da, vecs, d):
    lp = r.shape[0]
    groups = min(4, d // RWKV_GROUP_W)
    chunks = 4
    width = groups * RWKV_GROUP_W
    rows = chunks * CHUNK
    act = pl.BlockSpec((rows, width), lambda j, c: (c, j))
    vec = pl.BlockSpec((1, width), lambda j, c: (0, j))
    return pl.pallas_call(
        functools.partial(_rwkv_kernel, groups=groups, chunks=chunks), grid=(d // width, lp // rows),
        in_specs=[act] * 6 + [vec] * 7, out_specs=act,
        out_shape=jax.ShapeDtypeStruct((lp, d), BF16),
        scratch_shapes=[pltpu.VMEM((groups, RWKV_GROUP_W, RWKV_GROUP_W), F32)],
        compiler_params=_cparams(2), name="rwkv7",
    )(r, k, v, gate, dw, da, *[t.reshape(1, d) for t in vecs])


def _hgrn2_layer(u, w_in, j, lb, gnorm_g, w_out, d):
    p = _matmul(u, w_in, j, 4 * d)
    return _matmul(_hgrn2_core(p, lb, gnorm_g, d), w_out, j, d, out_dtype=BF16)


def _attn_layer(u, w_in, j, lam4, subln_g, w_out, d, lambda_init, tables):
    p = _matmul(u, w_in, j, 4 * d, out_dtype=BF16)
    qk = _rope_qk(p, *tables, d)
    return _matmul(_attn_core(qk, p, lam4, subln_g, d, lambda_init), w_out, j, d)


def _rwkv_layer(xs, w_in, j, w0, w1, w2, a0, a1, a2, k_k, k_a, r_k, lnx_w, lnx_b, w_out, d):
    r, k, v, gate = (_matmul(xs, w_in, j, d, col_off=s * d, lead=s) for s in range(4))
    dw = _lora(xs, 4, w1.astype(BF16), w2.astype(BF16), True)
    da = _lora(xs, 5, a1.astype(BF16), a2.astype(BF16), False)
    o = _rwkv_core(r, k, v, gate, dw, da, (w0, a0, k_k, k_a, r_k.reshape(-1), lnx_w, lnx_b), d)
    return _matmul(o, w_out, j, d, out_dtype=BF16)


def kernel(x, meta_tokens, pre_norm_g, post_norm_g, a_w_in, a_lb_logits, a_gnorm_g, a_w_out, b_w_in, b_lam_q1, b_lam_k1, b_lam_q2, b_lam_k2, b_subln_g, b_w_out, c_mu, c_w_in, c_w0, c_w1, c_w2, c_a0, c_a1, c_a2, c_k_k, c_k_a, c_r_k, c_lnx_w, c_lnx_b, c_w_out):
    bsz, seq, d = x.shape
    depth = pre_norm_g.shape[0]
    assert bsz == 1 and seq % ATTN_TILE == 0 and meta_tokens.shape[0] == N_META and depth >= 2
    lp = ROW0 + seq
    h = (jnp.concatenate([jnp.zeros((ROW0 - N_META, d), F32), meta_tokens.astype(F32)], axis=0), x[0])
    tables = _rope_tables(lp)
    lb_p = jax.nn.softmax(a_lb_logits.astype(F32), axis=0)
    lb_all = jnp.cumsum(lb_p, axis=0) - lb_p[0]

    u = _prenorm(h, pre_norm_g[0], BF16)
    for i in range(depth):
        kind, j = i % N_MIXERS, i // N_MIXERS
        if kind == 0:
            y = _hgrn2_layer(u, a_w_in, j, lb_all[j], a_gnorm_g[j], a_w_out, d)
        elif kind == 1:
            lambda_init = 0.8 - 0.6 * math.exp(-0.3 * i)
            lam4 = jnp.stack([b_lam_q1[j], b_lam_k1[j], b_lam_q2[j], b_lam_k2[j]]).astype(F32)
            y = _attn_layer(u, b_w_in, j, lam4, b_subln_g[j], b_w_out, d, lambda_init, tables)
        else:
            y = _rwkv_layer(u, c_w_in, j, c_w0[j], c_w1[j], c_w2[j], c_a0[j], c_a1[j], c_a2[j],
                            c_k_k[j], c_k_a[j], c_r_k[j], c_lnx_w[j], c_lnx_b[j], c_w_out, d)
        if i + 1 == depth:
            out = _residual_out(h[0], y, post_norm_g[i], seq)
        elif (i + 1) % N_MIXERS == 2:
            hn, u = _residual_shiftmix(h[0], y, post_norm_g[i], pre_norm_g[i + 1], c_mu[(i + 1) // N_MIXERS])
            h = (hn,)
        else:
            hn, u = _residual_prenorm(h, y, post_norm_g[i], pre_norm_g[i + 1], BF16)
            h = (hn,)
    return out[None]
```
